```python
import functools
import jax, jax.numpy as jnp
from jax import lax
import numpy as np

D_MODEL = 1024
BATCH = 32
SEQ = 256
DEPTH = 4
DEC_BATCH = 4
DEC_SEQ = 4096
PAST_LEN = 256

GRID_W = 64
D_MIX = D_MODEL
A_W = D_MIX // 4
B_W = D_MIX // 4
B_GROUPS = 4
B_HD = B_W // B_GROUPS
CHUNK = 128
C_W = D_MIX // 2
C_HEADS = 8
C_HD = C_W // C_HEADS
WIN_ROWS = 8
WIN_COLS = 16
Q_BLOCK = WIN_COLS
K_BLOCK = 2 * WIN_COLS
N_COL_BLOCKS = GRID_W // Q_BLOCK
CONV_K = 3
ALPHA = (2 * DEPTH) ** 0.25
BETA = (8 * DEPTH) ** -0.25
ATTN_SCALE = C_HD ** -0.5
NEG_INF = -1e30
_BRANCH_WIDTHS = (A_W, A_W, A_W, A_W, B_W, B_W, B_W, C_W, C_W, C_W, C_W)
IN_W = sum(_BRANCH_WIDTHS)
SPLIT_POINTS = tuple(sum(_BRANCH_WIDTHS[:i + 1]) for i in range(len(_BRANCH_WIDTHS) - 1))

kernel_name = "hybrid_dit_conv_gmlp_natten_step"


def _layernorm(x, eps=1e-5):
    xf = x.astype(jnp.float32)
    mu = jnp.mean(xf, axis=-1, keepdims=True)
    var = jnp.mean(jnp.square(xf - mu), axis=-1, keepdims=True)
    return ((xf - mu) * lax.rsqrt(var + eps)).astype(x.dtype)


def _short_conv(xa, ba, ca, conv_w):
    z = ca * xa
    zp = jnp.pad(z, ((0, 0), (1, 1), (0, 0)))
    conv = conv_w[0] * zp[:, :-2] + conv_w[1] * zp[:, 1:-1] + conv_w[2] * zp[:, 2:]
    return ba * conv


def _chunk_gmlp(u, v, ln_g, ln_b, w_s, b_s):
    bn, n, _ = v.shape
    v = _layernorm(v) * ln_g + ln_b
    v = v.reshape(bn, n // CHUNK, CHUNK, B_GROUPS, B_HD)
    sv = jnp.einsum('gpq,bnqgc->bnpgc', w_s, v) + b_s.T[:, :, None]
    return u * sv.reshape(bn, n, B_W)


def _ctx_attention(q, k, v):
    s = jnp.einsum('bqhd,bkhd->bhqk', q, k, preferred_element_type=jnp.float32) * ATTN_SCALE
    p = jax.nn.softmax(s, axis=-1).astype(v.dtype)
    return jnp.einsum('bhqk,bkhd->bqhd', p, v)


def _neighbourhood_attention(q, k, v, ctx_k, ctx_v, rpb):
    bn, n = q.shape[:2]
    rows = n // GRID_W
    wr = min(WIN_ROWS, rows)
    r = jnp.arange(rows)
    row_start = jnp.clip(r - wr // 2, 0, rows - wr)
    row_idx = row_start[:, None] + jnp.arange(wr)
    qcol = jnp.arange(N_COL_BLOCKS)[:, None] * Q_BLOCK + jnp.arange(Q_BLOCK)
    blk_start = jnp.clip(jnp.arange(N_COL_BLOCKS) * Q_BLOCK - WIN_COLS // 2, 0, GRID_W - K_BLOCK)
    col_idx = blk_start[:, None] + jnp.arange(K_BLOCK)

    ri = row_idx[:, None, :, None]
    ci = col_idx[None, :, None, :]
    kg = k.reshape(bn, rows, GRID_W, C_HEADS, C_HD)
    vg = v.reshape(bn, rows, GRID_W, C_HEADS, C_HD)
    kb = kg[:, ri, ci].reshape(bn, rows, N_COL_BLOCKS, wr * K_BLOCK, C_HEADS, C_HD)
    vb = vg[:, ri, ci].reshape(bn, rows, N_COL_BLOCKS, wr * K_BLOCK, C_HEADS, C_HD)
    qb = q.reshape(bn, rows, N_COL_BLOCKS, Q_BLOCK, C_HEADS, C_HD)

    dr_idx = row_idx - r[:, None] + (WIN_ROWS - 1)
    dc = col_idx[:, None, :] - qcol[:, :, None]
    dc_idx = jnp.clip(dc, -(WIN_COLS - 1), WIN_COLS - 1) + (WIN_COLS - 1)
    bias = rpb[:, dr_idx[:, None, None, :, None], dc_idx[None, :, :, None, :]]
    bias = jnp.moveaxis(bias, 0, 2).reshape(rows, N_COL_BLOCKS, C_HEADS, Q_BLOCK, wr * K_BLOCK)
    col_start = jnp.clip(qcol - WIN_COLS // 2, 0, GRID_W - WIN_COLS)
    valid = (col_idx[:, None, :] >= col_start[..., None]) & (col_idx[:, None, :] < col_start[..., None] + WIN_COLS)
    valid = jnp.broadcast_to(valid[:, :, None, :], (N_COL_BLOCKS, Q_BLOCK, wr, K_BLOCK))
    valid = valid.reshape(N_COL_BLOCKS, 1, Q_BLOCK, wr * K_BLOCK)

    s_loc = jnp.einsum('brjqhd,brjkhd->brjhqk', qb, kb, preferred_element_type=jnp.float32) * ATTN_SCALE
    s_loc = jnp.where(valid, s_loc + bias, NEG_INF)
    s_ctx = jnp.einsum('brjqhd,blhd->brjhql', qb, ctx_k, preferred_element_type=jnp.float32) * ATTN_SCALE
    p = jax.nn.softmax(jnp.concatenate([s_loc, s_ctx], axis=-1), axis=-1).astype(v.dtype)
    p_loc, p_ctx = p[..., :wr * K_BLOCK], p[..., wr * K_BLOCK:]
    out = (jnp.einsum('brjhqk,brjkhd->brjqhd', p_loc, vb)
           + jnp.einsum('brjhql,blhd->brjqhd', p_ctx, ctx_v))
    return out.reshape(bn, n, C_HEADS, C_HD)


def _layer(x, mod, w_in_l, conv_w_l, gln_g, gln_b, ws_l, bs_l, w_out_l, ln_g_l, ln_b_l, attn_fn):
    shift, scale, gate = jnp.split(mod, 3, axis=-1)
    h = _layernorm(x) * (1 + scale[:, None]) + shift[:, None]
    proj = h @ w_in_l
    xa, ba, ca, ga, u, v, gb, q, k, vv, gc = jnp.split(proj, SPLIT_POINTS, axis=-1)
    bn, n, _ = x.shape
    ya = _short_conv(xa, ba, ca, conv_w_l) * jax.nn.silu(ga)
    yb = _chunk_gmlp(u, v, gln_g, gln_b, ws_l, bs_l) * jax.nn.silu(gb)
    q = q.reshape(bn, n, C_HEADS, C_HD)
    k = k.reshape(bn, n, C_HEADS, C_HD)
    vv = vv.reshape(bn, n, C_HEADS, C_HD)
    yc = attn_fn(q, k, vv).reshape(bn, n, C_W) * jax.nn.silu(gc)
    out = jnp.concatenate([ya, yb, yc], axis=-1) @ w_out_l
    x_new = _layernorm(ALPHA * x + gate[:, None] * out) * ln_g_l + ln_b_l
    return x_new, k, vv


def setup_inputs(seed: int = 0) -> dict:
    key = jax.random.key(seed)
    ks = jax.random.split(key, 20)
    nrm = jax.random.normal
    d = D_MODEL
    return {
        "x_prompt": nrm(ks[0], (BATCH, SEQ, d), jnp.float32),
        "x_sample": nrm(ks[1], (DEC_BATCH, DEC_SEQ, d), jnp.float32),
        "cache_k": nrm(ks[2], (DEC_BATCH, DEPTH, PAST_LEN, C_HEADS, C_HD), jnp.float32),
        "cache_v": nrm(ks[3], (DEC_BATCH, DEPTH, PAST_LEN, C_HEADS, C_HD), jnp.float32),
        "c": nrm(ks[4], (DEC_BATCH, d), jnp.float32),
        "c_ctx": nrm(ks[5], (d,), jnp.float32),
        "w_ada": nrm(ks[6], (DEPTH, d, 3 * d), jnp.float32) * (0.5 * d ** -0.5),
        "b_ada": 0.01 * nrm(ks[7], (DEPTH, 3 * d), jnp.float32),
        "w_in": nrm(ks[8], (DEPTH, d, IN_W), jnp.float32) * d ** -0.5,
        "conv_w": nrm(ks[9], (DEPTH, CONV_K, A_W), jnp.float32) * CONV_K ** -0.5,
        "gmlp_ln_g": 1.0 + 0.05 * nrm(ks[10], (DEPTH, B_W), jnp.float32),
        "gmlp_ln_b": 0.02 * nrm(ks[11], (DEPTH, B_W), jnp.float32),
        "w_spatial": nrm(ks[12], (DEPTH, B_GROUPS, CHUNK, CHUNK), jnp.float32) * CHUNK ** -0.5,
        "b_spatial": 1.0 + 0.02 * nrm(ks[13], (DEPTH, B_GROUPS, CHUNK), jnp.float32),
        "rpb": 0.1 * nrm(ks[14], (DEPTH, C_HEADS, 2 * WIN_ROWS - 1, 2 * WIN_COLS - 1), jnp.float32),
        "w_out": nrm(ks[15], (DEPTH, D_MIX, d), jnp.float32) * (D_MIX ** -0.5 * BETA),
        "ln_g": 1.0 + 0.05 * nrm(ks[16], (DEPTH, d), jnp.float32),
        "ln_b": 0.02 * nrm(ks[17], (DEPTH, d), jnp.float32),
    }


def reference(x_prompt, x_sample, cache_k, cache_v, c, c_ctx, w_ada, b_ada, w_in, conv_w,
              gmlp_ln_g, gmlp_ln_b, w_spatial, b_spatial, rpb, w_out, ln_g, ln_b):
    xp = x_prompt
    xs = x_sample
    silu_ctx = jax.nn.silu(c_ctx)[None]
    silu_c = jax.nn.silu(c)
    new_k = []
    new_v = []
    for l in range(DEPTH):
        weights_l = (w_in[l], conv_w[l], gmlp_ln_g[l], gmlp_ln_b[l], w_spatial[l], b_spatial[l],
                     w_out[l], ln_g[l], ln_b[l])
        mod_ctx = silu_ctx @ w_ada[l] + b_ada[l]
        xp, k_ctx, v_ctx = _layer(xp, mod_ctx, *weights_l, _ctx_attention)
        new_k.append(k_ctx)
        new_v.append(v_ctx)
        mod_lat = silu_c @ w_ada[l] + b_ada[l]
        attn_lat = functools.partial(_neighbourhood_attention, ctx_k=cache_k[:, l],
                                     ctx_v=cache_v[:, l], rpb=rpb[l])
        xs, _, _ = _layer(xs, mod_lat, *weights_l, attn_lat)
    new_k_arr = jnp.stack(new_k, axis=1)
    new_v_arr = jnp.stack(new_v, axis=1)
    return (xp, xs, new_k_arr, new_v_arr)
```

```python
import functools

import jax
import jax.numpy as jnp
from jax import lax
from jax.experimental import pallas as pl
from jax.experimental.pallas import tpu as pltpu

F32 = jnp.float32
BF16 = jnp.bfloat16

D_MODEL = 1024
DEPTH = 4
GRID_W = 64
A_W = 256
B_W = 256
CHUNK = 128
C_W = 512
C_HEADS = 8
C_HD = 64
WIN_ROWS = 8
WIN_COLS = 16
IN_W = 3840
ALPHA = (2 * DEPTH) ** 0.25
ATTN_SCALE = C_HD ** -0.5
NEG_INF = -1e30
LN_EPS = 1e-5

LANES = 128
N_HEAD_PAIRS = C_W // LANES
A_COLS = (0, 4 * A_W)
B_COLS = (4 * A_W, 4 * A_W + 3 * B_W)
C_COLS = (4 * A_W + 3 * B_W, IN_W)

TOKEN_BLOCK = 512
GROUP_ROWS = 4
WINDOW_ROWS = 12
N_BIAS_ENTRIES = 18
VMEM_LIMIT_BYTES = 56 * 1024 * 1024


def _layernorm(x):
    mu = jnp.mean(x, axis=-1, keepdims=True)
    xc = x - mu
    var = jnp.mean(xc * xc, axis=-1, keepdims=True)
    return xc * lax.rsqrt(var + LN_EPS)


def _silu(x):
    return x / (1.0 + jnp.exp(-x))


def _lane_is_low(shape):
    return lax.broadcasted_iota(jnp.int32, shape, len(shape) - 1) < C_HD


def _mod_kernel(c_ref, w_ref, b_ref, o_ref):
    s = _silu(c_ref[...])
    o_ref[0] = jnp.dot(s, w_ref[0], precision=lax.Precision.HIGHEST,
                       preferred_element_type=F32) + b_ref[0]


def _modulation(cvec, w_ada, b_ada):
    n_rows = cvec.shape[0]
    col_block = 1024
    return pl.pallas_call(
        _mod_kernel,
        grid=(DEPTH, 3 * D_MODEL // col_block),
        in_specs=[
            pl.BlockSpec((n_rows, D_MODEL), lambda l, j: (0, 0)),
            pl.BlockSpec((1, D_MODEL, col_block), lambda l, j: (l, 0, j)),
            pl.BlockSpec((1, 1, col_block), lambda l, j: (l, 0, j)),
        ],
        out_specs=pl.BlockSpec((1, n_rows, col_block), lambda l, j: (l, 0, j)),
        out_shape=jax.ShapeDtypeStruct((DEPTH, n_rows, 3 * D_MODEL), F32),
        compiler_params=pltpu.CompilerParams(
            dimension_semantics=("arbitrary", "arbitrary"), vmem_limit_bytes=VMEM_LIMIT_BYTES),
        name="adaln_modulation",
    )(cvec, w_ada, b_ada.reshape(DEPTH, 1, 3 * D_MODEL))


def _modulated_norm(x, mod_ref):
    shift = mod_ref[0, :, 0:D_MODEL]
    scale = mod_ref[0, :, D_MODEL:2 * D_MODEL]
    return (_layernorm(x) * (1.0 + scale) + shift).astype(BF16)


def _conv_branch(h, h_halo, pos, seq_len, w_in_ref, conv_w_ref):
    tb = h.shape[0]
    pa = jnp.dot(h, w_in_ref[:, A_COLS[0]:A_COLS[1]], preferred_element_type=F32)
    xa, ba = pa[:, 0:A_W], pa[:, A_W:2 * A_W]
    ca, ga = pa[:, 2 * A_W:3 * A_W], pa[:, 3 * A_W:4 * A_W]
    z = ca * xa
    z_prev = pltpu.roll(z, 1, 0)
    z_next = pltpu.roll(z, tb - 1, 0)
    if h_halo is not None:
        xh = jnp.dot(h_halo, w_in_ref[:, 0:A_W], preferred_element_type=F32)
        ch = jnp.dot(h_halo, w_in_ref[:, 2 * A_W:3 * A_W], preferred_element_type=F32)
        zh = ch * xh
        row = lax.broadcasted_iota(jnp.int32, (tb, 1), 0)
        z_prev = jnp.where(row == 0, zh[7:8, :], z_prev)
        z_next = jnp.where(row == tb - 1, zh[8:9, :], z_next)
    z_prev = jnp.where(pos == 0, 0.0, z_prev)
    z_next = jnp.where(pos == seq_len - 1, 0.0, z_next)
    cw = conv_w_ref[...]
    conv = cw[0:1, :] * z_prev + cw[1:2, :] * z + cw[2:3, :] * z_next
    return ba * conv * _silu(ga)


def _gmlp_branch(h, w_in_ref, gln_g_ref, gln_b_ref, ws_ref, bs_ref):
    tb = h.shape[0]
    pb = jnp.dot(h, w_in_ref[:, B_COLS[0]:B_COLS[1]], preferred_element_type=F32)
    u, v, gb = pb[:, 0:B_W], pb[:, B_W:2 * B_W], pb[:, 2 * B_W:3 * B_W]
    vn = _layernorm(v) * gln_g_ref[...] + gln_b_ref[...]
    low = _lane_is_low((CHUNK, LANES))
    chunks = []
    for c in range(tb // CHUNK):
        tiles = []
        for t in range(B_W // LANES):
            vt = vn[c * CHUNK:(c + 1) * CHUNK, t * LANES:(t + 1) * LANES]
            rhs = jnp.concatenate([jnp.where(low, vt, 0.0), jnp.where(low, 0.0, vt)],
                                  axis=0).astype(BF16)
            tiles.append(jnp.dot(ws_ref[t], rhs, preferred_element_type=F32) + bs_ref[t])
        chunks.append(jnp.concatenate(tiles, axis=1))
    sv = jnp.concatenate(chunks, axis=0)
    return u * sv * _silu(gb)


def _post_norm(x, y_bf16, gate, w_out_ref, ln_g_ref, ln_b_ref):
    out = jnp.dot(y_bf16, w_out_ref[...], preferred_element_type=F32)
    return _layernorm(ALPHA * x + gate * out) * ln_g_ref[...] + ln_b_ref[...]


def _ctx_layer_kernel(seq_len, x_ref, mod_ref, w_in_ref, conv_w_ref, gln_g_ref, gln_b_ref, ws_ref,
                      bs_ref, w_out_ref, ln_g_ref, ln_b_ref, *rest):
    xo_ref, ko_ref, vo_ref, q_scr, k_scr, v_scr, o_scr = rest[-7:]
    tb = x_ref.shape[0]
    n_seq = tb // seq_len
    x = x_ref[...]
    h = _modulated_norm(x, mod_ref)
    pos = lax.broadcasted_iota(jnp.int32, (tb, 1), 0) & (seq_len - 1)
    ya = _conv_branch(h, None, pos, seq_len, w_in_ref, conv_w_ref)
    yb = _gmlp_branch(h, w_in_ref, gln_g_ref, gln_b_ref, ws_ref, bs_ref)

    pc = jnp.dot(h, w_in_ref[:, C_COLS[0]:C_COLS[1]], preferred_element_type=F32)
    q, k = pc[:, 0:C_W], pc[:, C_W:2 * C_W]
    v, gc = pc[:, 2 * C_W:3 * C_W], pc[:, 3 * C_W:4 * C_W]
    for s in range(n_seq):
        ko_ref[s, 0] = k[s * seq_len:(s + 1) * seq_len, :]
        vo_ref[s, 0] = v[s * seq_len:(s + 1) * seq_len, :]
    qs = q * ATTN_SCALE
    for t in range(N_HEAD_PAIRS):
        q_scr[t] = qs[:, t * LANES:(t + 1) * LANES]
        k_scr[t] = k[:, t * LANES:(t + 1) * LANES].astype(BF16)
        v_scr[t] = v[:, t * LANES:(t + 1) * LANES]

    low = _lane_is_low((seq_len, LANES))

    def head_pair(t, carry):
        for s in range(n_seq):
            rows = pl.ds(s * seq_len, seq_len)
            q_t = q_scr[t, rows, :]
            k_t = k_scr[t, rows, :]
            v_t = v_scr[t, rows, :]
            acc = jnp.zeros((seq_len, LANES), F32)
            for parity in range(2):
                keep = low if parity == 0 else jnp.logical_not(low)
                q_h = jnp.where(keep, q_t, 0.0).astype(BF16)
                v_h = jnp.where(keep, v_t, 0.0).astype(BF16)
                sc = lax.dot_general(q_h, k_t, (((1,), (1,)), ((), ())),
                                     preferred_element_type=F32)
                m = jnp.max(sc, axis=-1, keepdims=True)
                p = jnp.exp(sc - m)
                denom = jnp.sum(p, axis=-1, keepdims=True)
                o = jnp.dot(p.astype(BF16), v_h, preferred_element_type=F32)
                acc = acc + o / denom
            o_scr[t, rows, :] = acc
        return carry

    lax.fori_loop(0, N_HEAD_PAIRS, head_pair, 0)

    attn = jnp.concatenate([o_scr[t] for t in range(N_HEAD_PAIRS)], axis=1)
    yc = attn * _silu(gc)
    y = jnp.concatenate([ya, yb, yc], axis=1).astype(BF16)
    gate = mod_ref[0, :, 2 * D_MODEL:3 * D_MODEL]
    xo_ref[...] = _post_norm(x, y, gate, w_out_ref, ln_g_ref, ln_b_ref)


def _const_spec(shape):
    return pl.BlockSpec(shape, lambda *_: (0,) * len(shape))


def _weight_specs():
    return [
        _const_spec((D_MODEL, IN_W)),
        _const_spec((3, A_W)),
        _const_spec((1, B_W)),
        _const_spec((1, B_W)),
        _const_spec((B_W // LANES, CHUNK, 2 * CHUNK)),
        _const_spec((B_W // LANES, CHUNK, LANES)),
    ]


def _ctx_layer(layer, xp, mod, lw, kv_prev):
    n_tok = xp.shape[0]
    seq_len = 256
    n_seq_total = n_tok // seq_len
    seq_per_block = TOKEN_BLOCK // seq_len
    kv_shape = jax.ShapeDtypeStruct((n_seq_total, DEPTH, seq_len, C_W), F32)
    kv_spec = pl.BlockSpec((seq_per_block, 1, seq_len, C_W), lambda i: (i, layer, 0, 0))
    in_specs = [
        pl.BlockSpec((TOKEN_BLOCK, D_MODEL), lambda i: (i, 0)),
        _const_spec((1, 1, 3 * D_MODEL)),
        *_weight_specs(),
        _const_spec((D_MODEL, D_MODEL)),
        _const_spec((1, D_MODEL)),
        _const_spec((1, D_MODEL)),
    ]
    args = [xp, mod, lw["w_in"], lw["conv_w"], lw["gln_g"], lw["gln_b"], lw["ws"], lw["bs"],
            lw["w_out"], lw["ln_g"], lw["ln_b"]]
    aliases = {}
    if kv_prev is not None:
        in_specs += [pl.BlockSpec(memory_space=pl.ANY), pl.BlockSpec(memory_space=pl.ANY)]
        aliases = {len(args): 1, len(args) + 1: 2}
        args += list(kv_prev)
    return pl.pallas_call(
        functools.partial(_ctx_layer_kernel, seq_len),
        grid=(n_tok // TOKEN_BLOCK,),
        in_specs=in_specs,
        out_specs=[pl.BlockSpec((TOKEN_BLOCK, D_MODEL), lambda i: (i, 0)), kv_spec, kv_spec],
        out_shape=[jax.ShapeDtypeStruct((n_tok, D_MODEL), F32), kv_shape, kv_shape],
        scratch_shapes=[
            pltpu.VMEM((N_HEAD_PAIRS, TOKEN_BLOCK, LANES), F32),
            pltpu.VMEM((N_HEAD_PAIRS, TOKEN_BLOCK, LANES), BF16),
            pltpu.VMEM((N_HEAD_PAIRS, TOKEN_BLOCK, LANES), F32),
            pltpu.VMEM((N_HEAD_PAIRS, TOKEN_BLOCK, LANES), F32),
        ],
        input_output_aliases=aliases,
        compiler_params=pltpu.CompilerParams(
            dimension_semantics=("arbitrary",), vmem_limit_bytes=VMEM_LIMIT_BYTES),
        name=f"ctx_layer_{layer}",
    )(*args)


def _lat_proj_kernel(seq_len, x_ref, xprev_ref, xnext_ref, mod_ref, w_in_ref, conv_w_ref, gln_g_ref,
                     gln_b_ref, ws_ref, bs_ref, yab_ref, q_ref, k_ref, v_ref, g_ref):
    tb = x_ref.shape[0]
    x = x_ref[...]
    h = _modulated_norm(x, mod_ref)
    h_halo = _modulated_norm(jnp.concatenate([xprev_ref[0], xnext_ref[0]], axis=0), mod_ref)
    pos = (pl.program_id(0) * tb + lax.broadcasted_iota(jnp.int32, (tb, 1), 0)) & (seq_len - 1)
    ya = _conv_branch(h, h_halo, pos, seq_len, w_in_ref, conv_w_ref)
    yb = _gmlp_branch(h, w_in_ref, gln_g_ref, gln_b_ref, ws_ref, bs_ref)
    yab_ref[...] = jnp.concatenate([ya, yb], axis=1).astype(BF16)

    pc = jnp.dot(h, w_in_ref[:, C_COLS[0]:C_COLS[1]], preferred_element_type=F32)
    qs = pc[:, 0:C_W] * ATTN_SCALE
    k, v = pc[:, C_W:2 * C_W], pc[:, 2 * C_W:3 * C_W]
    for t in range(N_HEAD_PAIRS):
        q_ref[t] = qs[:, t * LANES:(t + 1) * LANES].astype(BF16)
        k_ref[t] = k[:, t * LANES:(t + 1) * LANES].astype(BF16)
        v_ref[t] = v[:, t * LANES:(t + 1) * LANES].astype(BF16)
    g_ref[...] = _silu(pc[:, 3 * C_W:4 * C_W]).astype(BF16)


def _lat_proj(layer, xs, mod, lw, seq_len):
    n_tok = xs.shape[0]
    blocks_per_seq = seq_len // TOKEN_BLOCK
    n_halo = n_tok // 8
    halo_per_block = TOKEN_BLOCK // 8
    xs8 = xs.reshape(n_halo, 8, D_MODEL)
    pair_shape = jax.ShapeDtypeStruct((N_HEAD_PAIRS, n_tok, LANES), BF16)
    pair_spec = pl.BlockSpec((N_HEAD_PAIRS, TOKEN_BLOCK, LANES), lambda i: (0, i, 0))
    return pl.pallas_call(
        functools.partial(_lat_proj_kernel, seq_len),
        grid=(n_tok // TOKEN_BLOCK,),
        in_specs=[
            pl.BlockSpec((TOKEN_BLOCK, D_MODEL), lambda i: (i, 0)),
            pl.BlockSpec((1, 8, D_MODEL), lambda i: (jnp.maximum(i * halo_per_block - 1, 0), 0, 0)),
            pl.BlockSpec((1, 8, D_MODEL),
                         lambda i: (jnp.minimum((i + 1) * halo_per_block, n_halo - 1), 0, 0)),
            pl.BlockSpec((1, 1, 3 * D_MODEL), lambda i: (i // blocks_per_seq, 0, 0)),
            *_weight_specs(),
        ],
        out_specs=[
            pl.BlockSpec((TOKEN_BLOCK, A_W + B_W), lambda i: (i, 0)),
            pair_spec, pair_spec, pair_spec,
            pl.BlockSpec((TOKEN_BLOCK, C_W), lambda i: (i, 0)),
        ],
        out_shape=[
            jax.ShapeDtypeStruct((n_tok, A_W + B_W), BF16),
            pair_shape, pair_shape, pair_shape,
            jax.ShapeDtypeStruct((n_tok, C_W), BF16),
        ],
        compiler_params=pltpu.CompilerParams(
            dimension_semantics=("arbitrary",), vmem_limit_bytes=VMEM_LIMIT_BYTES),
        name=f"lat_proj_{layer}",
    )(xs, xs8, xs8, mod, lw["w_in"], lw["conv_w"], lw["gln_g"], lw["gln_b"], lw["ws"], lw["bs"])


def _bias_index(e, jp, is_first, is_last):
    interior = jnp.where(e == 2, 16, jnp.where(e == 10, 17, jnp.where((e < 2) | (e >= 11), 15, e)))
    edge = jnp.clip(e, 0, 14)
    if jp >= 4:
        edge = jnp.where(is_first, 15, edge)
    if jp < 2:
        edge = jnp.where(is_last, 15, edge)
    return jnp.where(is_first | is_last, edge, interior)


def _lat_attn_kernel(x_ref, yab_ref, q_ref, g_ref, k_ref, v_ref, ck_ref, cv_ref, tbl_ref, mod_ref,
                     w_out_ref, ln_g_ref, ln_b_ref, xo_ref, o_scr):
    rb = pl.program_id(1)
    hp = pl.program_id(2)
    tb = x_ref.shape[0]
    group_tokens = GROUP_ROWS * GRID_W
    window_tokens = WINDOW_ROWS * GRID_W
    n_groups = (4096 // GRID_W) // GROUP_ROWS
    groups_per_block = tb // group_tokens
    low_q = _lane_is_low((group_tokens, LANES))
    low_w = _lane_is_low((window_tokens, LANES))
    low_c = _lane_is_low((ck_ref.shape[2], LANES))
    ck = ck_ref[0, 0]
    cv = cv_ref[0, 0]

    for gi in range(groups_per_block):
        g = rb * groups_per_block + gi
        u0 = jnp.clip(GROUP_ROWS * g - WIN_ROWS // 2, 0, GRID_W - WINDOW_ROWS)
        base = u0 - GROUP_ROWS * g + (WIN_ROWS - 1)
        is_first = g == 0
        is_last = g == n_groups - 1
        start = pl.multiple_of(u0 * GRID_W, GRID_W)
        kw = k_ref[0, pl.ds(start, window_tokens), :]
        vw = v_ref[0, pl.ds(start, window_tokens), :]
        qg = q_ref[0, gi * group_tokens:(gi + 1) * group_tokens, :]
        acc = jnp.zeros((group_tokens, LANES), F32)
        for parity in range(2):
            head = 2 * hp + parity
            keep_q = low_q if parity == 0 else jnp.logical_not(low_q)
            keep_w = low_w if parity == 0 else jnp.logical_not(low_w)
            keep_c = low_c if parity == 0 else jnp.logical_not(low_c)
            q_h = jnp.where(keep_q, qg, jnp.zeros_like(qg))
            v_h = jnp.where(keep_w, vw, jnp.zeros_like(vw))
            cv_h = jnp.where(keep_c, cv, jnp.zeros_like(cv))
            s_loc = lax.dot_general(q_h, kw, (((1,), (1,)), ((), ())), preferred_element_type=F32)
            s_ctx = lax.dot_general(q_h, ck, (((1,), (1,)), ((), ())), preferred_element_type=F32)
            bias_rows = []
            for i in range(GROUP_ROWS):
                tiles = []
                for jp in range(WINDOW_ROWS // 2):
                    idx = _bias_index(base + 2 * jp - i, jp, is_first, is_last)
                    tiles.append(tbl_ref[head * N_BIAS_ENTRIES + idx])
                bias_rows.append(jnp.concatenate(tiles, axis=1))
            s_loc = s_loc + jnp.concatenate(bias_rows, axis=0)
            m = jnp.maximum(jnp.max(s_loc, axis=-1, keepdims=True),
                            jnp.max(s_ctx, axis=-1, keepdims=True))
            p_loc = jnp.exp(s_loc - m)
            p_ctx = jnp.exp(s_ctx - m)
            denom = jnp.sum(p_loc, axis=-1, keepdims=True) + jnp.sum(p_ctx, axis=-1, keepdims=True)
            o = (jnp.dot(p_loc.astype(BF16), v_h, preferred_element_type=F32)
                 + jnp.dot(p_ctx.astype(BF16), cv_h, preferred_element_type=F32))
            acc = acc + o / denom
        o_scr[hp, gi * group_tokens:(gi + 1) * group_tokens, :] = acc

    @pl.when(hp == N_HEAD_PAIRS - 1)
    def _():
        attn = jnp.concatenate([o_scr[t] for t in range(N_HEAD_PAIRS)], axis=1)
        yc = (attn * g_ref[...].astype(F32)).astype(BF16)
        y = jnp.concatenate([yab_ref[...], yc], axis=1)
        gate = mod_ref[0, :, 2 * D_MODEL:3 * D_MODEL]
        xo_ref[...] = _post_norm(x_ref[...], y, gate, w_out_ref, ln_g_ref, ln_b_ref)


def _lat_attn(layer, xs, yab, q, k, v, gc, ck, cv, tbl, mod, lw, seq_len):
    n_tok = xs.shape[0]
    n_batch = n_tok // seq_len
    blocks_per_seq = seq_len // TOKEN_BLOCK
    past = ck.shape[2]
    tok_spec = lambda width: pl.BlockSpec((TOKEN_BLOCK, width),
                                          lambda b, r, t: (b * blocks_per_seq + r, 0))
    return pl.pallas_call(
        _lat_attn_kernel,
        grid=(n_batch, blocks_per_seq, N_HEAD_PAIRS),
        in_specs=[
            tok_spec(D_MODEL),
            tok_spec(A_W + B_W),
            pl.BlockSpec((1, TOKEN_BLOCK, LANES), lambda b, r, t: (t, b * blocks_per_seq + r, 0)),
            tok_spec(C_W),
            pl.BlockSpec((1, seq_len, LANES), lambda b, r, t: (t, b, 0)),
            pl.BlockSpec((1, seq_len, LANES), lambda b, r, t: (t, b, 0)),
            pl.BlockSpec((1, 1, past, LANES), lambda b, r, t: (t, b, 0, 0)),
            pl.BlockSpec((1, 1, past, LANES), lambda b, r, t: (t, b, 0, 0)),
            _const_spec(tbl.shape),
            pl.BlockSpec((1, 1, 3 * D_MODEL), lambda b, r, t: (b, 0, 0)),
            _const_spec((D_MODEL, D_MODEL)),
            _const_spec((1, D_MODEL)),
            _const_spec((1, D_MODEL)),
        ],
        out_specs=tok_spec(D_MODEL),
        out_shape=jax.ShapeDtypeStruct((n_tok, D_MODEL), F32),
        scratch_shapes=[pltpu.VMEM((N_HEAD_PAIRS, TOKEN_BLOCK, LANES), F32)],
        compiler_params=pltpu.CompilerParams(
            dimension_semantics=("arbitrary", "arbitrary", "arbitrary"),
            vmem_limit_bytes=VMEM_LIMIT_BYTES),
        name=f"lat_attn_{layer}",
    )(xs, yab, q, gc, k, v, ck, cv, tbl, mod, lw["w_out"], lw["ln_g"], lw["ln_b"])


def _bias_table(rpb_l):
    c = jnp.arange(GRID_W)[:, None]
    cp = jnp.arange(GRID_W)[None, :]
    dc = jnp.clip(cp - c, -(WIN_COLS - 1), WIN_COLS - 1) + (WIN_COLS - 1)
    cs = jnp.clip(c - WIN_COLS // 2, 0, GRID_W - WIN_COLS)
    valid = (cp >= cs) & (cp < cs + WIN_COLS)
    t = jnp.where(valid[None, None], rpb_l[:, :, dc], NEG_INF)
    neg = jnp.full((C_HEADS, 1, GRID_W, GRID_W), NEG_INF, F32)
    tx = jnp.concatenate([t, neg], axis=1)
    pairs = jnp.concatenate([tx[:, 0:15], tx[:, 1:16]], axis=-1)
    masked = jnp.concatenate([neg, neg], axis=-1)
    lo_masked = jnp.concatenate([neg, t[:, 3:4]], axis=-1)
    hi_masked = jnp.concatenate([t[:, 10:11], neg], axis=-1)
    tbl = jnp.concatenate([pairs, masked, lo_masked, hi_masked], axis=1)
    return tbl.reshape(C_HEADS * N_BIAS_ENTRIES, GRID_W, 2 * GRID_W)


def _layer_weights(l, w_in_bf, w_out_bf, conv_w, gmlp_ln_g, gmlp_ln_b, ws_pairs, bs_lanes, ln_g, ln_b):
    return {
        "w_in": w_in_bf[l], "w_out": w_out_bf[l], "conv_w": conv_w[l],
        "gln_g": gmlp_ln_g[l][None], "gln_b": gmlp_ln_b[l][None],
        "ws": ws_pairs[l], "bs": bs_lanes[l],
        "ln_g": ln_g[l][None], "ln_b": ln_b[l][None],
    }


def kernel(x_prompt, x_sample, cache_k, cache_v, c, c_ctx, w_ada, b_ada, w_in, conv_w, gmlp_ln_g,
           gmlp_ln_b, w_spatial, b_spatial, rpb, w_out, ln_g, ln_b):
    batch, seq, d = x_prompt.shape
    dec_batch, dec_seq, _ = x_sample.shape
    past = cache_k.shape[2]

    n_mod_rows = 8
    cvec = jnp.concatenate(
        [c_ctx[None], c, jnp.zeros((n_mod_rows - 1 - dec_batch, d), F32)], axis=0)
    mod = _modulation(cvec, w_ada, b_ada)

    w_in_bf = w_in.astype(BF16)
    w_out_bf = w_out.astype(BF16)
    ws_pairs = w_spatial.reshape(DEPTH, 2, 2, CHUNK, CHUNK).transpose(0, 1, 3, 2, 4)
    ws_pairs = ws_pairs.reshape(DEPTH, 2, CHUNK, 2 * CHUNK).astype(BF16)
    bs_lanes = jnp.repeat(b_spatial.reshape(DEPTH, 2, 2, CHUNK).transpose(0, 1, 3, 2), C_HD, axis=-1)

    def pair_major(cache_l):
        return cache_l.reshape(dec_batch, past, N_HEAD_PAIRS, LANES).transpose(2, 0, 1, 3).astype(BF16)

    xp = x_prompt.reshape(batch * seq, d)
    xs = x_sample.reshape(dec_batch * dec_seq, d)
    kv = None
    for l in range(DEPTH):
        lw = _layer_weights(l, w_in_bf, w_out_bf, conv_w, gmlp_ln_g, gmlp_ln_b, ws_pairs, bs_lanes,
                            ln_g, ln_b)
        xp, new_k, new_v = _ctx_layer(l, xp, mod[l, 0:1][None], lw, kv)
        kv = (new_k, new_v)
        mod_lat = mod[l, 1:1 + dec_batch][:, None]
        yab, q, k, v, gc = _lat_proj(l, xs, mod_lat, lw, dec_seq)
        xs = _lat_attn(l, xs, yab, q, k, v, gc, pair_major(cache_k[:, l]), pair_major(cache_v[:, l]),
                       _bias_table(rpb[l]), mod_lat, lw, dec_seq)

    new_k, new_v = kv
    return (xp.reshape(batch, seq, d), xs.reshape(dec_batch, dec_seq, d),
            new_k.reshape(batch, DEPTH, seq, C_HEADS, C_HD),
            new_v.reshape(batch, DEPTH, seq, C_HEADS, C_HD))
```

```python
import functools

import jax
import jax.numpy as jnp
from jax import lax
from jax.experimental import pallas as pl
from jax.experimental.pallas import tpu as pltpu

F32 = jnp.float32
BF16 = jnp.bfloat16

D_MODEL = 1024
DEPTH = 4
GRID_W = 64
A_W = 256
B_W = 256
CHUNK = 128
C_W = 512
C_HEADS = 8
C_HD = 64
WIN_ROWS = 8
WIN_COLS = 16
IN_W = 3840
ALPHA = (2 * DEPTH) ** 0.25
ATTN_SCALE = C_HD ** -0.5
NEG_INF = -1e30
LN_EPS = 1e-5
LOG2_E = 1.4426950408889634

LANES = 128
N_HEAD_PAIRS = C_W // LANES
A_COLS = (0, 4 * A_W)
B_COLS = (4 * A_W, 4 * A_W + 3 * B_W)
C_COLS = (4 * A_W + 3 * B_W, IN_W)

TOKEN_BLOCK = 512
GROUP_ROWS = 4
WINDOW_ROWS = 12
ENTRY_MASKED = 16
ENTRY_LOW_ONLY = 17
ENTRY_HIGH_ONLY = 18
N_BIAS_ENTRIES = 19
VMEM_LIMIT_BYTES = 56 * 1024 * 1024


def _layernorm(x):
    mu = jnp.mean(x, axis=-1, keepdims=True)
    xc = x - mu
    var = jnp.mean(xc * xc, axis=-1, keepdims=True)
    return xc * lax.rsqrt(var + LN_EPS)


def _silu(x):
    return x / (1.0 + jnp.exp(-x))


def _lane_is_low(shape):
    return lax.broadcasted_iota(jnp.int32, shape, len(shape) - 1) < C_HD


def _mod_kernel(c_ref, w_ref, b_ref, o_ref):
    s = _silu(c_ref[...])
    o_ref[0] = jnp.dot(s, w_ref[0], precision=lax.Precision.HIGHEST,
                       preferred_element_type=F32) + b_ref[0]


def _modulation(cvec, w_ada, b_ada):
    n_rows = cvec.shape[0]
    col_block = 1024
    return pl.pallas_call(
        _mod_kernel,
        grid=(DEPTH, 3 * D_MODEL // col_block),
        in_specs=[
            pl.BlockSpec((n_rows, D_MODEL), lambda l, j: (0, 0)),
            pl.BlockSpec((1, D_MODEL, col_block), lambda l, j: (l, 0, j)),
            pl.BlockSpec((1, 1, col_block), lambda l, j: (l, 0, j)),
        ],
        out_specs=pl.BlockSpec((1, n_rows, col_block), lambda l, j: (l, 0, j)),
        out_shape=jax.ShapeDtypeStruct((DEPTH, n_rows, 3 * D_MODEL), F32),
        compiler_params=pltpu.CompilerParams(
            dimension_semantics=("arbitrary", "arbitrary"), vmem_limit_bytes=VMEM_LIMIT_BYTES),
        name="adaln_modulation",
    )(cvec, w_ada, b_ada.reshape(DEPTH, 1, 3 * D_MODEL))


def _modulated_norm(x, mod_ref):
    shift = mod_ref[0, :, 0:D_MODEL]
    scale = mod_ref[0, :, D_MODEL:2 * D_MODEL]
    return (_layernorm(x) * (1.0 + scale) + shift).astype(BF16)


def _conv_branch(h, h_halo, pos, seq_len, w_in_ref, conv_w_ref):
    tb = h.shape[0]
    pa = jnp.dot(h, w_in_ref[:, A_COLS[0]:A_COLS[1]], preferred_element_type=F32)
    xa, ba = pa[:, 0:A_W], pa[:, A_W:2 * A_W]
    ca, ga = pa[:, 2 * A_W:3 * A_W], pa[:, 3 * A_W:4 * A_W]
    z = ca * xa
    z_prev = pltpu.roll(z, 1, 0)
    z_next = pltpu.roll(z, tb - 1, 0)
    if h_halo is not None:
        xh = jnp.dot(h_halo, w_in_ref[:, 0:A_W], preferred_element_type=F32)
        ch = jnp.dot(h_halo, w_in_ref[:, 2 * A_W:3 * A_W], preferred_element_type=F32)
        zh = ch * xh
        row = lax.broadcasted_iota(jnp.int32, (tb, 1), 0)
        z_prev = jnp.where(row == 0, zh[7:8, :], z_prev)
        z_next = jnp.where(row == tb - 1, zh[8:9, :], z_next)
    z_prev = jnp.where(pos == 0, 0.0, z_prev)
    z_next = jnp.where(pos == seq_len - 1, 0.0, z_next)
    cw = conv_w_ref[...]
    conv = cw[0:1, :] * z_prev + cw[1:2, :] * z + cw[2:3, :] * z_next
    return ba * conv * _silu(ga)


def _gmlp_branch(h, w_in_ref, gln_g_ref, gln_b_ref, ws_ref, bs_ref):
    tb = h.shape[0]
    pb = jnp.dot(h, w_in_ref[:, B_COLS[0]:B_COLS[1]], preferred_element_type=F32)
    u, v, gb = pb[:, 0:B_W], pb[:, B_W:2 * B_W], pb[:, 2 * B_W:3 * B_W]
    vn = _layernorm(v) * gln_g_ref[...] + gln_b_ref[...]
    low = _lane_is_low((CHUNK, LANES))
    chunks = []
    for c in range(tb // CHUNK):
        tiles = []
        for t in range(B_W // LANES):
            vt = vn[c * CHUNK:(c + 1) * CHUNK, t * LANES:(t + 1) * LANES]
            rhs = jnp.concatenate([jnp.where(low, vt, 0.0), jnp.where(low, 0.0, vt)],
                                  axis=0).astype(BF16)
            tiles.append(jnp.dot(ws_ref[t], rhs, preferred_element_type=F32) + bs_ref[t])
        chunks.append(jnp.concatenate(tiles, axis=1))
    sv = jnp.concatenate(chunks, axis=0)
    return u * sv * _silu(gb)


def _post_norm(x, y_bf16, gate, w_out_ref, ln_g_ref, ln_b_ref):
    out = jnp.dot(y_bf16, w_out_ref[...], preferred_element_type=F32)
    return _layernorm(ALPHA * x + gate * out) * ln_g_ref[...] + ln_b_ref[...]


def _ctx_layer_kernel(seq_len, x_ref, mod_ref, w_in_ref, conv_w_ref, gln_g_ref, gln_b_ref, ws_ref,
                      bs_ref, w_out_ref, ln_g_ref, ln_b_ref, *rest):
    xo_ref, ko_ref, vo_ref, q_scr, k_scr, v_scr, o_scr = rest[-7:]
    tb = x_ref.shape[0]
    n_seq = tb // seq_len
    x = x_ref[...]
    h = _modulated_norm(x, mod_ref)
    pos = lax.broadcasted_iota(jnp.int32, (tb, 1), 0) & (seq_len - 1)
    ya = _conv_branch(h, None, pos, seq_len, w_in_ref, conv_w_ref)
    yb = _gmlp_branch(h, w_in_ref, gln_g_ref, gln_b_ref, ws_ref, bs_ref)

    pc = jnp.dot(h, w_in_ref[:, C_COLS[0]:C_COLS[1]], preferred_element_type=F32)
    q, k = pc[:, 0:C_W], pc[:, C_W:2 * C_W]
    v, gc = pc[:, 2 * C_W:3 * C_W], pc[:, 3 * C_W:4 * C_W]
    for s in range(n_seq):
        ko_ref[s, 0] = k[s * seq_len:(s + 1) * seq_len, :]
        vo_ref[s, 0] = v[s * seq_len:(s + 1) * seq_len, :]
    qs = q * ATTN_SCALE
    for t in range(N_HEAD_PAIRS):
        q_scr[t] = qs[:, t * LANES:(t + 1) * LANES]
        k_scr[t] = k[:, t * LANES:(t + 1) * LANES].astype(BF16)
        v_scr[t] = v[:, t * LANES:(t + 1) * LANES]

    low = _lane_is_low((seq_len, LANES))

    def head_pair(t, carry):
        for s in range(n_seq):
            rows = pl.ds(s * seq_len, seq_len)
            q_t = q_scr[t, rows, :]
            k_t = k_scr[t, rows, :]
            v_t = v_scr[t, rows, :]
            acc = jnp.zeros((seq_len, LANES), F32)
            for parity in range(2):
                keep = low if parity == 0 else jnp.logical_not(low)
                q_h = jnp.where(keep, q_t, 0.0).astype(BF16)
                v_h = jnp.where(keep, v_t, 0.0).astype(BF16)
                sc = lax.dot_general(q_h, k_t, (((1,), (1,)), ((), ())),
                                     preferred_element_type=F32)
                m = jnp.max(sc, axis=-1, keepdims=True)
                p = jnp.exp(sc - m)
                denom = jnp.sum(p, axis=-1, keepdims=True)
                o = jnp.dot(p.astype(BF16), v_h, preferred_element_type=F32)
                acc = acc + o / denom
            o_scr[t, rows, :] = acc
        return carry

    lax.fori_loop(0, N_HEAD_PAIRS, head_pair, 0)

    attn = jnp.concatenate([o_scr[t] for t in range(N_HEAD_PAIRS)], axis=1)
    yc = attn * _silu(gc)
    y = jnp.concatenate([ya, yb, yc], axis=1).astype(BF16)
    gate = mod_ref[0, :, 2 * D_MODEL:3 * D_MODEL]
    xo_ref[...] = _post_norm(x, y, gate, w_out_ref, ln_g_ref, ln_b_ref)


def _const_spec(shape):
    return pl.BlockSpec(shape, lambda *_: (0,) * len(shape))


def _weight_specs():
    return [
        _const_spec((D_MODEL, IN_W)),
        _const_spec((3, A_W)),
        _const_spec((1, B_W)),
        _const_spec((1, B_W)),
        _const_spec((B_W // LANES, CHUNK, 2 * CHUNK)),
        _const_spec((B_W // LANES, CHUNK, LANES)),
    ]


def _ctx_layer(layer, xp, mod, lw, kv_prev):
    n_tok = xp.shape[0]
    seq_len = 256
    n_seq_total = n_tok // seq_len
    seq_per_block = TOKEN_BLOCK // seq_len
    kv_shape = jax.ShapeDtypeStruct((n_seq_total, DEPTH, seq_len, C_W), F32)
    kv_spec = pl.BlockSpec((seq_per_block, 1, seq_len, C_W), lambda i: (i, layer, 0, 0))
    in_specs = [
        pl.BlockSpec((TOKEN_BLOCK, D_MODEL), lambda i: (i, 0)),
        _const_spec((1, 1, 3 * D_MODEL)),
        *_weight_specs(),
        _const_spec((D_MODEL, D_MODEL)),
        _const_spec((1, D_MODEL)),
        _const_spec((1, D_MODEL)),
    ]
    args = [xp, mod, lw["w_in"], lw["conv_w"], lw["gln_g"], lw["gln_b"], lw["ws"], lw["bs"],
            lw["w_out"], lw["ln_g"], lw["ln_b"]]
    aliases = {}
    if kv_prev is not None:
        in_specs += [pl.BlockSpec(memory_space=pl.ANY), pl.BlockSpec(memory_space=pl.ANY)]
        aliases = {len(args): 1, len(args) + 1: 2}
        args += list(kv_prev)
    return pl.pallas_call(
        functools.partial(_ctx_layer_kernel, seq_len),
        grid=(n_tok // TOKEN_BLOCK,),
        in_specs=in_specs,
        out_specs=[pl.BlockSpec((TOKEN_BLOCK, D_MODEL), lambda i: (i, 0)), kv_spec, kv_spec],
        out_shape=[jax.ShapeDtypeStruct((n_tok, D_MODEL), F32), kv_shape, kv_shape],
        scratch_shapes=[
            pltpu.VMEM((N_HEAD_PAIRS, TOKEN_BLOCK, LANES), F32),
            pltpu.VMEM((N_HEAD_PAIRS, TOKEN_BLOCK, LANES), BF16),
            pltpu.VMEM((N_HEAD_PAIRS, TOKEN_BLOCK, LANES), F32),
            pltpu.VMEM((N_HEAD_PAIRS, TOKEN_BLOCK, LANES), F32),
        ],
        input_output_aliases=aliases,
        compiler_params=pltpu.CompilerParams(
            dimension_semantics=("arbitrary",), vmem_limit_bytes=VMEM_LIMIT_BYTES),
        name=f"ctx_layer_{layer}",
    )(*args)


def _lat_proj_kernel(seq_len, x_ref, xprev_ref, xnext_ref, mod_ref, w_in_ref, conv_w_ref, gln_g_ref,
                     gln_b_ref, ws_ref, bs_ref, yab_ref, q_ref, k_ref, vt_ref, g_ref):
    tb = x_ref.shape[0]
    x = x_ref[...]
    h = _modulated_norm(x, mod_ref)
    h_halo = _modulated_norm(jnp.concatenate([xprev_ref[0], xnext_ref[0]], axis=0), mod_ref)
    pos = (pl.program_id(0) * tb + lax.broadcasted_iota(jnp.int32, (tb, 1), 0)) & (seq_len - 1)
    ya = _conv_branch(h, h_halo, pos, seq_len, w_in_ref, conv_w_ref)
    yb = _gmlp_branch(h, w_in_ref, gln_g_ref, gln_b_ref, ws_ref, bs_ref)
    yab_ref[...] = jnp.concatenate([ya, yb], axis=1).astype(BF16)

    pc = jnp.dot(h, w_in_ref[:, C_COLS[0]:C_COLS[1]], preferred_element_type=F32)
    qs = pc[:, 0:C_W] * (ATTN_SCALE * LOG2_E)
    k, v = pc[:, C_W:2 * C_W], pc[:, 2 * C_W:3 * C_W]
    for t in range(N_HEAD_PAIRS):
        q_ref[t] = qs[:, t * LANES:(t + 1) * LANES].astype(BF16)
        k_ref[t] = k[:, t * LANES:(t + 1) * LANES].astype(BF16)
        for n in range(tb // LANES):
            vt_ref[t, n] = v[n * LANES:(n + 1) * LANES, t * LANES:(t + 1) * LANES].T.astype(BF16)
    g_ref[...] = _silu(pc[:, 3 * C_W:4 * C_W]).astype(BF16)


def _lat_proj(layer, xs, mod, lw, seq_len):
    n_tok = xs.shape[0]
    blocks_per_seq = seq_len // TOKEN_BLOCK
    n_halo = n_tok // 8
    halo_per_block = TOKEN_BLOCK // 8
    xs8 = xs.reshape(n_halo, 8, D_MODEL)
    pair_shape = jax.ShapeDtypeStruct((N_HEAD_PAIRS, n_tok, LANES), BF16)
    pair_spec = pl.BlockSpec((N_HEAD_PAIRS, TOKEN_BLOCK, LANES), lambda i: (0, i, 0))
    return pl.pallas_call(
        functools.partial(_lat_proj_kernel, seq_len),
        grid=(n_tok // TOKEN_BLOCK,),
        in_specs=[
            pl.BlockSpec((TOKEN_BLOCK, D_MODEL), lambda i: (i, 0)),
            pl.BlockSpec((1, 8, D_MODEL), lambda i: (jnp.maximum(i * halo_per_block - 1, 0), 0, 0)),
            pl.BlockSpec((1, 8, D_MODEL),
                         lambda i: (jnp.minimum((i + 1) * halo_per_block, n_halo - 1), 0, 0)),
            pl.BlockSpec((1, 1, 3 * D_MODEL), lambda i: (i // blocks_per_seq, 0, 0)),
            *_weight_specs(),
        ],
        out_specs=[
            pl.BlockSpec((TOKEN_BLOCK, A_W + B_W), lambda i: (i, 0)),
            pair_spec, pair_spec,
            pl.BlockSpec((N_HEAD_PAIRS, TOKEN_BLOCK // LANES, LANES, LANES), lambda i: (0, i, 0, 0)),
            pl.BlockSpec((TOKEN_BLOCK, C_W), lambda i: (i, 0)),
        ],
        out_shape=[
            jax.ShapeDtypeStruct((n_tok, A_W + B_W), BF16),
            pair_shape, pair_shape,
            jax.ShapeDtypeStruct((N_HEAD_PAIRS, n_tok // LANES, LANES, LANES), BF16),
            jax.ShapeDtypeStruct((n_tok, C_W), BF16),
        ],
        compiler_params=pltpu.CompilerParams(
            dimension_semantics=("arbitrary",), vmem_limit_bytes=VMEM_LIMIT_BYTES),
        name=f"lat_proj_{layer}",
    )(xs, xs8, xs8, mod, lw["w_in"], lw["conv_w"], lw["gln_g"], lw["gln_b"], lw["ws"], lw["bs"])


def _bias_index(e, j, is_first, is_last):
    interior = jnp.where(e == 3, ENTRY_LOW_ONLY,
                         jnp.where(e == 11, ENTRY_HIGH_ONLY,
                                   jnp.where((e < 3) | (e > 11), ENTRY_MASKED, e)))
    edge = jnp.clip(e, 0, 15)
    if j >= WIN_ROWS:
        edge = jnp.where(is_first, ENTRY_MASKED, edge)
    if j < WINDOW_ROWS - WIN_ROWS:
        edge = jnp.where(is_last, ENTRY_MASKED, edge)
    return jnp.where(is_first | is_last, edge, interior)


def _lat_attn_kernel(x_ref, yab_ref, q_ref, g_ref, k_ref, vt_ref, ck_ref, cvt_ref, tbl_ref, mod_ref,
                     w_out_ref, ln_g_ref, ln_b_ref, xo_ref, s_scr, m_scr, ot_scr):
    rb = pl.program_id(1)
    tb = x_ref.shape[0]
    group_tokens = GROUP_ROWS * GRID_W
    window_tokens = WINDOW_ROWS * GRID_W
    window_tiles = window_tokens // LANES
    n_groups = (k_ref.shape[1] // GRID_W) // GROUP_ROWS
    groups_per_block = tb // group_tokens
    n_units = N_HEAD_PAIRS * groups_per_block * 2
    nt = (((1,), (1,)), ((), ()))

    def coords(u):
        hp = lax.shift_right_logical(u, 2)
        gi = lax.shift_right_logical(u, 1) & 1
        parity = u & 1
        g = rb * groups_per_block + gi
        first_tile = jnp.clip(2 * g - 2, 0, (GRID_W - WINDOW_ROWS) // 2)
        return hp, gi, parity, g, first_tile

    def score_stage(u, slot):
        hp, gi, parity, g, first_tile = coords(u)
        base = 2 * first_tile - GROUP_ROWS * g + (WIN_ROWS - 1)
        is_first = g == 0
        is_last = g == n_groups - 1
        start = pl.multiple_of(first_tile * LANES, LANES)
        kw = k_ref[hp, pl.ds(start, window_tokens), :]
        qg = q_ref[hp, pl.ds(pl.multiple_of(gi * group_tokens, group_tokens), group_tokens), :]
        lane_half = lax.shift_right_logical(
            lax.broadcasted_iota(jnp.int32, (group_tokens, LANES), 1), 6)
        q_h = jnp.where(lane_half == parity, qg, jnp.zeros_like(qg))
        half = window_tokens // 2
        s_loc = jnp.concatenate(
            [lax.dot_general(kw[0:half], q_h, nt, preferred_element_type=F32),
             lax.dot_general(kw[half:], q_h, nt, preferred_element_type=F32)], axis=0)
        s_ctx = lax.dot_general(ck_ref[hp, 0], q_h, nt, preferred_element_type=F32)
        head = 2 * hp + parity
        bias_rows = []
        for j in range(WINDOW_ROWS):
            tiles = []
            for ip in range(GROUP_ROWS // 2):
                idx = _bias_index(base + j - 2 * ip, j, is_first, is_last)
                tiles.append(tbl_ref[head * N_BIAS_ENTRIES + idx])
            bias_rows.append(jnp.concatenate(tiles, axis=1))
        s_loc = s_loc + jnp.concatenate(bias_rows, axis=0)
        s_scr[slot, 0:window_tokens, :] = s_loc
        s_scr[slot, window_tokens:, :] = s_ctx
        m_scr[slot] = jnp.maximum(jnp.max(s_loc, axis=0, keepdims=True),
                                  jnp.max(s_ctx, axis=0, keepdims=True))

    def value_stage(u, slot):
        hp, gi, parity, g, first_tile = coords(u)
        p = jnp.exp2(s_scr[slot] - m_scr[slot]).astype(BF16)
        rows = pl.ds(pl.multiple_of(parity * C_HD, C_HD), C_HD)
        vt_tiles = vt_ref[hp, pl.ds(first_tile, window_tiles), rows, :]
        vtw = jnp.concatenate([vt_tiles[n] for n in range(window_tiles)], axis=1)
        ones_rows = 16
        lhs_loc = jnp.concatenate([vtw, jnp.ones((ones_rows, window_tokens), BF16)], axis=0)
        lhs_ctx = jnp.concatenate(
            [cvt_ref[hp, 0, rows, :], jnp.ones((ones_rows, s_scr.shape[1] - window_tokens), BF16)],
            axis=0)
        o_t = (jnp.dot(lhs_loc, p[0:window_tokens], preferred_element_type=F32)
               + jnp.dot(lhs_ctx, p[window_tokens:], preferred_element_type=F32))
        ot_scr[hp, gi, rows, :] = o_t[0:C_HD] / o_t[C_HD:C_HD + 1]

    score_stage(jnp.int32(0), 0)

    def pipeline_step(i, carry):
        score_stage(2 * i + 1, 1)
        value_stage(2 * i, 0)
        score_stage(jnp.minimum(2 * i + 2, n_units - 1), 0)
        value_stage(2 * i + 1, 1)
        return carry

    lax.fori_loop(0, n_units // 2, pipeline_step, 0)

    gate = mod_ref[0, :, 2 * D_MODEL:3 * D_MODEL]
    for gi in range(groups_per_block):
        rows = slice(gi * group_tokens, (gi + 1) * group_tokens)
        attn = jnp.concatenate([ot_scr[t, gi].T for t in range(N_HEAD_PAIRS)], axis=1)
        yc = (attn * g_ref[rows, :].astype(F32)).astype(BF16)
        y = jnp.concatenate([yab_ref[rows, :], yc], axis=1)
        xo_ref[rows, :] = _post_norm(x_ref[rows, :], y, gate, w_out_ref, ln_g_ref, ln_b_ref)


def _lat_attn(layer, xs, yab, q, k, vt, gc, ck, cvt, tbl, mod, lw, seq_len):
    n_tok = xs.shape[0]
    n_batch = n_tok // seq_len
    blocks_per_seq = seq_len // TOKEN_BLOCK
    past = ck.shape[2]
    group_tokens = GROUP_ROWS * GRID_W
    tok_spec = lambda width: pl.BlockSpec((TOKEN_BLOCK, width),
                                          lambda b, r: (b * blocks_per_seq + r, 0))
    return pl.pallas_call(
        _lat_attn_kernel,
        grid=(n_batch, blocks_per_seq),
        in_specs=[
            tok_spec(D_MODEL),
            tok_spec(A_W + B_W),
            pl.BlockSpec((N_HEAD_PAIRS, TOKEN_BLOCK, LANES), lambda b, r: (0, b * blocks_per_seq + r, 0)),
            tok_spec(C_W),
            pl.BlockSpec((N_HEAD_PAIRS, seq_len, LANES), lambda b, r: (0, b, 0)),
            pl.BlockSpec((N_HEAD_PAIRS, seq_len // LANES, LANES, LANES), lambda b, r: (0, b, 0, 0)),
            pl.BlockSpec((N_HEAD_PAIRS, 1, past, LANES), lambda b, r: (0, b, 0, 0)),
            pl.BlockSpec((N_HEAD_PAIRS, 1, LANES, past), lambda b, r: (0, b, 0, 0)),
            _const_spec(tbl.shape),
            pl.BlockSpec((1, 1, 3 * D_MODEL), lambda b, r: (b, 0, 0)),
            _const_spec((D_MODEL, D_MODEL)),
            _const_spec((1, D_MODEL)),
            _const_spec((1, D_MODEL)),
        ],
        out_specs=tok_spec(D_MODEL),
        out_shape=jax.ShapeDtypeStruct((n_tok, D_MODEL), F32),
        scratch_shapes=[
            pltpu.VMEM((2, WINDOW_ROWS * GRID_W + past, group_tokens), F32),
            pltpu.VMEM((2, 1, group_tokens), F32),
            pltpu.VMEM((N_HEAD_PAIRS, TOKEN_BLOCK // group_tokens, LANES, group_tokens), F32),
        ],
        compiler_params=pltpu.CompilerParams(
            dimension_semantics=("arbitrary", "arbitrary"),
            vmem_limit_bytes=VMEM_LIMIT_BYTES),
        name=f"lat_attn_{layer}",
    )(xs, yab, q, gc, k, vt, ck, cvt, tbl, mod, lw["w_out"], lw["ln_g"], lw["ln_b"])


def _bias_table_kernel(rp_ref, o_ref):
    n_taps = 2 * WIN_COLS - 1
    n_dr = 2 * WIN_ROWS - 1
    cp = lax.broadcasted_iota(jnp.int32, (GRID_W, LANES), 0)
    lane = lax.broadcasted_iota(jnp.int32, (GRID_W, LANES), 1)
    c = lane & (GRID_W - 1)
    cs = jnp.clip(c - WIN_COLS // 2, 0, GRID_W - WIN_COLS)
    valid = (cp >= cs) & (cp < cs + WIN_COLS)
    low = lane < GRID_W
    neg = jnp.full((GRID_W, LANES), NEG_INF, F32)

    def diagonal(e, lane_offset):
        row = jnp.broadcast_to(rp_ref[0, 0, e:e + 1, :], (GRID_W, LANES))
        shift = (LANES - (n_taps // 2) + lane_offset) % LANES
        return pltpu.roll(row, shift, 1, stride=1, stride_axis=0)

    def pair(e_low, e_high):
        lo = neg if e_low is None else diagonal(e_low, 0)
        hi = neg if e_high is None else diagonal(e_high, GRID_W)
        return jnp.where(valid, jnp.where(low, lo, hi) * LOG2_E, NEG_INF)

    for e in range(n_dr + 1):
        o_ref[0, 0, e] = pair(e if e < n_dr else None, e - 1 if e >= 1 else None)
    o_ref[0, 0, ENTRY_MASKED] = neg
    o_ref[0, 0, ENTRY_LOW_ONLY] = pair(3, None)
    o_ref[0, 0, ENTRY_HIGH_ONLY] = pair(None, 10)


def _bias_tables(rpb):
    n_dr, n_taps = rpb.shape[2], rpb.shape[3]
    rp = jnp.pad(rpb[..., ::-1], ((0, 0), (0, 0), (0, 16 - n_dr), (0, LANES - n_taps)),
                 constant_values=NEG_INF)
    tbl = pl.pallas_call(
        _bias_table_kernel,
        grid=(DEPTH, C_HEADS),
        in_specs=[pl.BlockSpec((1, 1, 16, LANES), lambda l, h: (l, h, 0, 0))],
        out_specs=pl.BlockSpec((1, 1, N_BIAS_ENTRIES, GRID_W, LANES), lambda l, h: (l, h, 0, 0, 0)),
        out_shape=jax.ShapeDtypeStruct((DEPTH, C_HEADS, N_BIAS_ENTRIES, GRID_W, LANES), F32),
        compiler_params=pltpu.CompilerParams(dimension_semantics=("arbitrary", "arbitrary")),
        name="bias_tables",
    )(rp)
    return tbl.reshape(DEPTH, C_HEADS * N_BIAS_ENTRIES, GRID_W, LANES)


def _layer_weights(l, w_in_bf, w_out_bf, conv_w, gmlp_ln_g, gmlp_ln_b, ws_pairs, bs_lanes, ln_g, ln_b):
    return {
        "w_in": w_in_bf[l], "w_out": w_out_bf[l], "conv_w": conv_w[l],
        "gln_g": gmlp_ln_g[l][None], "gln_b": gmlp_ln_b[l][None],
        "ws": ws_pairs[l], "bs": bs_lanes[l],
        "ln_g": ln_g[l][None], "ln_b": ln_b[l][None],
    }


def kernel(x_prompt, x_sample, cache_k, cache_v, c, c_ctx, w_ada, b_ada, w_in, conv_w, gmlp_ln_g,
           gmlp_ln_b, w_spatial, b_spatial, rpb, w_out, ln_g, ln_b):
    batch, seq, d = x_prompt.shape
    dec_batch, dec_seq, _ = x_sample.shape
    past = cache_k.shape[2]

    n_mod_rows = 8
    cvec = jnp.concatenate(
        [c_ctx[None], c, jnp.zeros((n_mod_rows - 1 - dec_batch, d), F32)], axis=0)
    mod = _modulation(cvec, w_ada, b_ada)

    w_in_bf = w_in.astype(BF16)
    w_out_bf = w_out.astype(BF16)
    ws_pairs = w_spatial.reshape(DEPTH, 2, 2, CHUNK, CHUNK).transpose(0, 1, 3, 2, 4)
    ws_pairs = ws_pairs.reshape(DEPTH, 2, CHUNK, 2 * CHUNK).astype(BF16)
    bs_lanes = jnp.repeat(b_spatial.reshape(DEPTH, 2, 2, CHUNK).transpose(0, 1, 3, 2), C_HD, axis=-1)

    def pair_major(cache):
        cache = cache.reshape(dec_batch, DEPTH, past, N_HEAD_PAIRS, LANES)
        return cache.transpose(1, 3, 0, 2, 4).astype(BF16)

    ck_all = pair_major(cache_k)
    cvt_all = jnp.swapaxes(pair_major(cache_v), -1, -2)
    tables = _bias_tables(rpb)

    xp = x_prompt.reshape(batch * seq, d)
    xs = x_sample.reshape(dec_batch * dec_seq, d)
    kv = None
    for l in range(DEPTH):
        lw = _layer_weights(l, w_in_bf, w_out_bf, conv_w, gmlp_ln_g, gmlp_ln_b, ws_pairs, bs_lanes,
                            ln_g, ln_b)
        xp, new_k, new_v = _ctx_layer(l, xp, mod[l, 0:1][None], lw, kv)
        kv = (new_k, new_v)
        mod_lat = mod[l, 1:1 + dec_batch][:, None]
        yab, q, k, vt, gc = _lat_proj(l, xs, mod_lat, lw, dec_seq)
        xs = _lat_attn(l, xs, yab, q, k, vt, gc, ck_all[l], cvt_all[l], tables[l], mod_lat, lw, dec_seq)

    new_k, new_v = kv
    return (xp.reshape(batch, seq, d), xs.reshape(dec_batch, dec_seq, d),
            new_k.reshape(batch, DEPTH, seq, C_HEADS, C_HD),
            new_v.reshape(batch, DEPTH, seq, C_HEADS, C_HD))
```

```python
import functools

import jax
import jax.numpy as jnp
from jax import lax
from jax.experimental import pallas as pl
from jax.experimental.pallas import tpu as pltpu

F32 = jnp.float32
BF16 = jnp.bfloat16

D_MODEL = 1024
DEPTH = 4
GRID_W = 64
A_W = 256
B_W = 256
CHUNK = 128
C_W = 512
C_HEADS = 8
C_HD = 64
WIN_ROWS = 8
WIN_COLS = 16
IN_W = 3840
ALPHA = (2 * DEPTH) ** 0.25
ATTN_SCALE = C_HD ** -0.5
NEG_INF = -1e30
LN_EPS = 1e-5
LOG2_E = 1.4426950408889634

LANES = 128
N_HEAD_PAIRS = C_W // LANES
A_COLS = (0, 4 * A_W)
B_COLS = (4 * A_W, 4 * A_W + 3 * B_W)
C_COLS = (4 * A_W + 3 * B_W, IN_W)

TOKEN_BLOCK = 512
ROW_PARTS = 2
GROUP_ROWS = 4
WINDOW_ROWS = 12
ENTRY_MASKED = 16
ENTRY_LOW_ONLY = 17
ENTRY_HIGH_ONLY = 18
N_BIAS_ENTRIES = 19
VMEM_LIMIT_BYTES = 56 * 1024 * 1024


def _layernorm(x):
    mu = jnp.mean(x, axis=-1, keepdims=True)
    xc = x - mu
    var = jnp.mean(xc * xc, axis=-1, keepdims=True)
    return xc * lax.rsqrt(var + LN_EPS)


def _silu(x):
    return x / (1.0 + jnp.exp(-x))


def _lane_is_low(shape):
    return lax.broadcasted_iota(jnp.int32, shape, len(shape) - 1) < C_HD


def _mod_kernel(c_ref, w_ref, b_ref, o_ref):
    s = _silu(c_ref[...])
    o_ref[0] = jnp.dot(s, w_ref[0], precision=lax.Precision.HIGHEST,
                       preferred_element_type=F32) + b_ref[0]


def _modulation(cvec, w_ada, b_ada):
    n_rows = cvec.shape[0]
    col_block = 1024
    return pl.pallas_call(
        _mod_kernel,
        grid=(DEPTH, 3 * D_MODEL // col_block),
        in_specs=[
            pl.BlockSpec((n_rows, D_MODEL), lambda l, j: (0, 0)),
            pl.BlockSpec((1, D_MODEL, col_block), lambda l, j: (l, 0, j)),
            pl.BlockSpec((1, 1, col_block), lambda l, j: (l, 0, j)),
        ],
        out_specs=pl.BlockSpec((1, n_rows, col_block), lambda l, j: (l, 0, j)),
        out_shape=jax.ShapeDtypeStruct((DEPTH, n_rows, 3 * D_MODEL), F32),
        compiler_params=pltpu.CompilerParams(
            dimension_semantics=("arbitrary", "arbitrary"), vmem_limit_bytes=VMEM_LIMIT_BYTES),
        name="adaln_modulation",
    )(cvec, w_ada, b_ada.reshape(DEPTH, 1, 3 * D_MODEL))


def _modulated_norm(x, mod_ref):
    shift = mod_ref[0, :, 0:D_MODEL]
    scale = mod_ref[0, :, D_MODEL:2 * D_MODEL]
    return (_layernorm(x) * (1.0 + scale) + shift).astype(BF16)


def _modulated_norm_parts(x_ref, mod_ref):
    rows = x_ref.shape[0] // ROW_PARTS
    return [_modulated_norm(x_ref[i * rows:(i + 1) * rows, :], mod_ref) for i in range(ROW_PARTS)]


def _project(h_parts, w_in_ref, cols):
    return jnp.concatenate(
        [jnp.dot(h, w_in_ref[:, cols[0]:cols[1]], preferred_element_type=F32) for h in h_parts],
        axis=0)


def _conv_halo(h_halo, w_in_ref):
    xh = jnp.dot(h_halo, w_in_ref[:, 0:A_W], preferred_element_type=F32)
    ch = jnp.dot(h_halo, w_in_ref[:, 2 * A_W:3 * A_W], preferred_element_type=F32)
    return ch * xh


def _conv_branch(pa, z_halo, pos, seq_len, conv_w_ref):
    tb = pa.shape[0]
    xa, ba = pa[:, 0:A_W], pa[:, A_W:2 * A_W]
    ca, ga = pa[:, 2 * A_W:3 * A_W], pa[:, 3 * A_W:4 * A_W]
    z = ca * xa
    z_prev = pltpu.roll(z, 1, 0)
    z_next = pltpu.roll(z, tb - 1, 0)
    if z_halo is not None:
        row = lax.broadcasted_iota(jnp.int32, (tb, 1), 0)
        z_prev = jnp.where(row == 0, z_halo[7:8, :], z_prev)
        z_next = jnp.where(row == tb - 1, z_halo[8:9, :], z_next)
    z_prev = jnp.where(pos == 0, 0.0, z_prev)
    z_next = jnp.where(pos == seq_len - 1, 0.0, z_next)
    cw = conv_w_ref[...]
    conv = cw[0:1, :] * z_prev + cw[1:2, :] * z + cw[2:3, :] * z_next
    return ba * conv * _silu(ga)


def _gmlp_branch(pb, gln_g_ref, gln_b_ref, ws_ref, bs_ref):
    tb = pb.shape[0]
    u, v, gb = pb[:, 0:B_W], pb[:, B_W:2 * B_W], pb[:, 2 * B_W:3 * B_W]
    vn = _layernorm(v) * gln_g_ref[...] + gln_b_ref[...]
    low = _lane_is_low((CHUNK, LANES))
    chunks = []
    for c in range(tb // CHUNK):
        tiles = []
        for t in range(B_W // LANES):
            vt = vn[c * CHUNK:(c + 1) * CHUNK, t * LANES:(t + 1) * LANES]
            rhs = jnp.concatenate([jnp.where(low, vt, 0.0), jnp.where(low, 0.0, vt)],
                                  axis=0).astype(BF16)
            tiles.append(jnp.dot(ws_ref[t], rhs, preferred_element_type=F32) + bs_ref[t])
        chunks.append(jnp.concatenate(tiles, axis=1))
    sv = jnp.concatenate(chunks, axis=0)
    return u * sv * _silu(gb)


def _post_norm(x, y_bf16, gate, w_out_ref, ln_g_ref, ln_b_ref):
    out = jnp.dot(y_bf16, w_out_ref[...], preferred_element_type=F32)
    return _layernorm(ALPHA * x + gate * out) * ln_g_ref[...] + ln_b_ref[...]


def _ctx_layer_kernel(seq_len, x_ref, mod_ref, w_in_ref, conv_w_ref, gln_g_ref, gln_b_ref, ws_ref,
                      bs_ref, w_out_ref, ln_g_ref, ln_b_ref, *rest):
    xo_ref, ko_ref, vo_ref, q_scr, k_scr, v_scr, o_scr = rest[-7:]
    tb = x_ref.shape[0]
    n_seq = tb // seq_len
    x = x_ref[...]
    h = _modulated_norm_parts(x_ref, mod_ref)
    pos = lax.broadcasted_iota(jnp.int32, (tb, 1), 0) & (seq_len - 1)
    pb = _project(h, w_in_ref, B_COLS)
    pa = _project(h, w_in_ref, A_COLS)
    yb = _gmlp_branch(pb, gln_g_ref, gln_b_ref, ws_ref, bs_ref)
    pc = _project(h, w_in_ref, C_COLS)
    ya = _conv_branch(pa, None, pos, seq_len, conv_w_ref)

    q, k = pc[:, 0:C_W], pc[:, C_W:2 * C_W]
    v, gc = pc[:, 2 * C_W:3 * C_W], pc[:, 3 * C_W:4 * C_W]
    for s in range(n_seq):
        ko_ref[s, 0] = k[s * seq_len:(s + 1) * seq_len, :]
        vo_ref[s, 0] = v[s * seq_len:(s + 1) * seq_len, :]
    qs = q * ATTN_SCALE
    for t in range(N_HEAD_PAIRS):
        q_scr[t] = qs[:, t * LANES:(t + 1) * LANES]
        k_scr[t] = k[:, t * LANES:(t + 1) * LANES].astype(BF16)
        v_scr[t] = v[:, t * LANES:(t + 1) * LANES]

    low = _lane_is_low((seq_len, LANES))

    def head_pair(t, carry):
        for s in range(n_seq):
            rows = pl.ds(s * seq_len, seq_len)
            q_t = q_scr[t, rows, :]
            k_t = k_scr[t, rows, :]
            v_t = v_scr[t, rows, :]
            acc = jnp.zeros((seq_len, LANES), F32)
            for parity in range(2):
                keep = low if parity == 0 else jnp.logical_not(low)
                q_h = jnp.where(keep, q_t, 0.0).astype(BF16)
                v_h = jnp.where(keep, v_t, 0.0).astype(BF16)
                sc = lax.dot_general(q_h, k_t, (((1,), (1,)), ((), ())),
                                     preferred_element_type=F32)
                m = jnp.max(sc, axis=-1, keepdims=True)
                p = jnp.exp(sc - m)
                denom = jnp.sum(p, axis=-1, keepdims=True)
                o = jnp.dot(p.astype(BF16), v_h, preferred_element_type=F32)
                acc = acc + o / denom
            o_scr[t, rows, :] = acc
        return carry

    lax.fori_loop(0, N_HEAD_PAIRS, head_pair, 0)

    attn = jnp.concatenate([o_scr[t] for t in range(N_HEAD_PAIRS)], axis=1)
    yc = attn * _silu(gc)
    y = jnp.concatenate([ya, yb, yc], axis=1).astype(BF16)
    gate = mod_ref[0, :, 2 * D_MODEL:3 * D_MODEL]
    xo_ref[...] = _post_norm(x, y, gate, w_out_ref, ln_g_ref, ln_b_ref)


def _const_spec(shape):
    return pl.BlockSpec(shape, lambda *_: (0,) * len(shape))


def _layer_spec(shape, layer):
    return pl.BlockSpec((None, *shape), lambda *_: (layer,) + (0,) * len(shape))


def _weight_specs(layer):
    return [
        _layer_spec((D_MODEL, IN_W), layer),
        _const_spec((3, A_W)),
        _const_spec((1, B_W)),
        _const_spec((1, B_W)),
        _const_spec((B_W // LANES, CHUNK, 2 * CHUNK)),
        _const_spec((B_W // LANES, CHUNK, LANES)),
    ]


def _ctx_layer(layer, xp, mod, lw, kv_prev):
    n_tok = xp.shape[0]
    seq_len = 256
    n_seq_total = n_tok // seq_len
    seq_per_block = TOKEN_BLOCK // seq_len
    kv_shape = jax.ShapeDtypeStruct((n_seq_total, DEPTH, seq_len, C_W), F32)
    kv_spec = pl.BlockSpec((seq_per_block, 1, seq_len, C_W), lambda i: (i, layer, 0, 0))
    in_specs = [
        pl.BlockSpec((TOKEN_BLOCK, D_MODEL), lambda i: (i, 0)),
        _const_spec((1, 1, 3 * D_MODEL)),
        *_weight_specs(layer),
        _layer_spec((D_MODEL, D_MODEL), layer),
        _const_spec((1, D_MODEL)),
        _const_spec((1, D_MODEL)),
    ]
    args = [xp, mod, lw["w_in"], lw["conv_w"], lw["gln_g"], lw["gln_b"], lw["ws"], lw["bs"],
            lw["w_out"], lw["ln_g"], lw["ln_b"]]
    aliases = {}
    if kv_prev is not None:
        in_specs += [pl.BlockSpec(memory_space=pl.ANY), pl.BlockSpec(memory_space=pl.ANY)]
        aliases = {len(args): 1, len(args) + 1: 2}
        args += list(kv_prev)
    return pl.pallas_call(
        functools.partial(_ctx_layer_kernel, seq_len),
        grid=(n_tok // TOKEN_BLOCK,),
        in_specs=in_specs,
        out_specs=[pl.BlockSpec((TOKEN_BLOCK, D_MODEL), lambda i: (i, 0)), kv_spec, kv_spec],
        out_shape=[jax.ShapeDtypeStruct((n_tok, D_MODEL), F32), kv_shape, kv_shape],
        scratch_shapes=[
            pltpu.VMEM((N_HEAD_PAIRS, TOKEN_BLOCK, LANES), F32),
            pltpu.VMEM((N_HEAD_PAIRS, TOKEN_BLOCK, LANES), BF16),
            pltpu.VMEM((N_HEAD_PAIRS, TOKEN_BLOCK, LANES), F32),
            pltpu.VMEM((N_HEAD_PAIRS, TOKEN_BLOCK, LANES), F32),
        ],
        input_output_aliases=aliases,
        compiler_params=pltpu.CompilerParams(
            dimension_semantics=("arbitrary",), vmem_limit_bytes=VMEM_LIMIT_BYTES),
        name=f"ctx_layer_{layer}",
    )(*args)


def _lat_proj_kernel(seq_len, x_ref, xprev_ref, xnext_ref, mod_ref, w_in_ref, conv_w_ref, gln_g_ref,
                     gln_b_ref, ws_ref, bs_ref, yab_ref, qt_ref, k_ref, vt_ref, g_ref):
    tb = x_ref.shape[0]
    group_tokens = GROUP_ROWS * GRID_W
    h = _modulated_norm_parts(x_ref, mod_ref)
    h_halo = _modulated_norm(jnp.concatenate([xprev_ref[0], xnext_ref[0]], axis=0), mod_ref)
    pos = (pl.program_id(0) * tb + lax.broadcasted_iota(jnp.int32, (tb, 1), 0)) & (seq_len - 1)
    z_halo = _conv_halo(h_halo, w_in_ref)
    pb = _project(h, w_in_ref, B_COLS)
    pa = _project(h, w_in_ref, A_COLS)
    yb = _gmlp_branch(pb, gln_g_ref, gln_b_ref, ws_ref, bs_ref)
    pvg = _project(h, w_in_ref, (C_COLS[0] + 2 * C_W, C_COLS[1]))
    ya = _conv_branch(pa, z_halo, pos, seq_len, conv_w_ref)
    yab_ref[...] = jnp.concatenate([ya, yb], axis=1).astype(BF16)
    pqk = _project(h, w_in_ref, (C_COLS[0], C_COLS[0] + 2 * C_W))
    v = pvg[:, 0:C_W]
    for t in range(N_HEAD_PAIRS):
        for n in range(tb // LANES):
            vt_ref[t, n] = v[n * LANES:(n + 1) * LANES, t * LANES:(t + 1) * LANES].T.astype(BF16)
    g_ref[...] = _silu(pvg[:, C_W:2 * C_W]).astype(BF16)
    qs = pqk[:, 0:C_W] * (ATTN_SCALE * LOG2_E)
    k = pqk[:, C_W:2 * C_W]
    for t in range(N_HEAD_PAIRS):
        k_ref[t] = k[:, t * LANES:(t + 1) * LANES].astype(BF16)
        for n in range(tb // group_tokens):
            qt_ref[t, n] = qs[n * group_tokens:(n + 1) * group_tokens,
                              t * LANES:(t + 1) * LANES].T.astype(BF16)


def _lat_proj(layer, xs, mod, lw, seq_len):
    n_tok = xs.shape[0]
    blocks_per_seq = seq_len // TOKEN_BLOCK
    n_halo = n_tok // 8
    halo_per_block = TOKEN_BLOCK // 8
    xs8 = xs.reshape(n_halo, 8, D_MODEL)
    group_tokens = GROUP_ROWS * GRID_W
    pair_shape = jax.ShapeDtypeStruct((N_HEAD_PAIRS, n_tok, LANES), BF16)
    pair_spec = pl.BlockSpec((N_HEAD_PAIRS, TOKEN_BLOCK, LANES), lambda i: (0, i, 0))
    return pl.pallas_call(
        functools.partial(_lat_proj_kernel, seq_len),
        grid=(n_tok // TOKEN_BLOCK,),
        in_specs=[
            pl.BlockSpec((TOKEN_BLOCK, D_MODEL), lambda i: (i, 0)),
            pl.BlockSpec((1, 8, D_MODEL), lambda i: (jnp.maximum(i * halo_per_block - 1, 0), 0, 0)),
            pl.BlockSpec((1, 8, D_MODEL),
                         lambda i: (jnp.minimum((i + 1) * halo_per_block, n_halo - 1), 0, 0)),
            pl.BlockSpec((1, 1, 3 * D_MODEL), lambda i: (i // blocks_per_seq, 0, 0)),
            *_weight_specs(layer),
        ],
        out_specs=[
            pl.BlockSpec((TOKEN_BLOCK, A_W + B_W), lambda i: (i, 0)),
            pl.BlockSpec((N_HEAD_PAIRS, TOKEN_BLOCK // group_tokens, LANES, group_tokens),
                         lambda i: (0, i, 0, 0)),
            pair_spec,
            pl.BlockSpec((N_HEAD_PAIRS, TOKEN_BLOCK // LANES, LANES, LANES), lambda i: (0, i, 0, 0)),
            pl.BlockSpec((TOKEN_BLOCK, C_W), lambda i: (i, 0)),
        ],
        out_shape=[
            jax.ShapeDtypeStruct((n_tok, A_W + B_W), BF16),
            jax.ShapeDtypeStruct((N_HEAD_PAIRS, n_tok // group_tokens, LANES, group_tokens), BF16),
            pair_shape,
            jax.ShapeDtypeStruct((N_HEAD_PAIRS, n_tok // LANES, LANES, LANES), BF16),
            jax.ShapeDtypeStruct((n_tok, C_W), BF16),
        ],
        compiler_params=pltpu.CompilerParams(
            dimension_semantics=("arbitrary",), vmem_limit_bytes=VMEM_LIMIT_BYTES),
        name=f"lat_proj_{layer}",
    )(xs, xs8, xs8, mod, lw["w_in"], lw["conv_w"], lw["gln_g"], lw["gln_b"], lw["ws"], lw["bs"])


def _bias_index(e, j, is_first, is_last):
    interior = jnp.where(e == 3, ENTRY_LOW_ONLY,
                         jnp.where(e == 11, ENTRY_HIGH_ONLY,
                                   jnp.where((e < 3) | (e > 11), ENTRY_MASKED, e)))
    edge = jnp.clip(e, 0, 15)
    if j >= WIN_ROWS:
        edge = jnp.where(is_first, ENTRY_MASKED, edge)
    if j < WINDOW_ROWS - WIN_ROWS:
        edge = jnp.where(is_last, ENTRY_MASKED, edge)
    return jnp.where(is_first | is_last, edge, interior)


def _lat_attn_kernel(x_ref, yab_ref, qt_ref, g_ref, k_ref, vt_ref, ck_ref, cvt_ref, tbl_ref, mod_ref,
                     w_out_ref, ln_g_ref, ln_b_ref, xo_ref, s_scr, m_scr, ot_scr):
    rb = pl.program_id(1)
    tb = x_ref.shape[0]
    group_tokens = GROUP_ROWS * GRID_W
    window_tokens = WINDOW_ROWS * GRID_W
    window_tiles = window_tokens // LANES
    n_groups = (k_ref.shape[1] // GRID_W) // GROUP_ROWS
    groups_per_block = tb // group_tokens
    n_units = N_HEAD_PAIRS * groups_per_block * 2
    past = ck_ref.shape[2]

    def coords(u):
        hp = lax.shift_right_logical(u, 2)
        gi = lax.shift_right_logical(u, 1) & 1
        parity = u & 1
        g = rb * groups_per_block + gi
        first_tile = jnp.clip(2 * g - 2, 0, (GRID_W - WINDOW_ROWS) // 2)
        return hp, gi, parity, g, first_tile

    def score_stage(u, slot):
        hp, gi, parity, g, first_tile = coords(u)
        base = 2 * first_tile - GROUP_ROWS * g + (WIN_ROWS - 1)
        is_first = g == 0
        is_last = g == n_groups - 1
        start = pl.multiple_of(first_tile * LANES, LANES)
        kw = k_ref[hp, pl.ds(start, window_tokens), :]
        qt = qt_ref[hp, gi]
        row_half = lax.shift_right_logical(
            lax.broadcasted_iota(jnp.int32, (LANES, group_tokens), 0), 6)
        qt_h = jnp.where(row_half == parity, qt, jnp.zeros_like(qt))
        half = (window_tokens + past) // 2
        s_a = jnp.dot(kw[0:half], qt_h, preferred_element_type=F32)
        s_b = jnp.dot(jnp.concatenate([kw[half:], ck_ref[hp, 0]], axis=0), qt_h,
                      preferred_element_type=F32)
        s_loc = jnp.concatenate([s_a, s_b[0:window_tokens - half]], axis=0)
        s_ctx = s_b[window_tokens - half:]
        head = 2 * hp + parity
        bias_rows = []
        for j in range(WINDOW_ROWS):
            tiles = []
            for ip in range(GROUP_ROWS // 2):
                idx = _bias_index(base + j - 2 * ip, j, is_first, is_last)
                tiles.append(tbl_ref[head * N_BIAS_ENTRIES + idx])
            bias_rows.append(jnp.concatenate(tiles, axis=1))
        s_loc = s_loc + jnp.concatenate(bias_rows, axis=0)
        s_scr[slot, 0:window_tokens, :] = s_loc
        s_scr[slot, window_tokens:, :] = s_ctx
        m_scr[slot] = jnp.maximum(jnp.max(s_loc, axis=0, keepdims=True),
                                  jnp.max(s_ctx, axis=0, keepdims=True))

    def value_stage(u, slot):
        hp, gi, parity, g, first_tile = coords(u)
        p = jnp.exp2(s_scr[slot] - m_scr[slot]).astype(BF16)
        rows = pl.ds(pl.multiple_of(parity * C_HD, C_HD), C_HD)
        vt_tiles = vt_ref[hp, pl.ds(first_tile, window_tiles), rows, :]
        vtw = jnp.concatenate([vt_tiles[n] for n in range(window_tiles)], axis=1)
        ones_rows = 16
        lhs_loc = jnp.concatenate([vtw, jnp.ones((ones_rows, window_tokens), BF16)], axis=0)
        lhs_ctx = jnp.concatenate(
            [cvt_ref[hp, 0, rows, :], jnp.ones((ones_rows, s_scr.shape[1] - window_tokens), BF16)],
            axis=0)
        o_t = (jnp.dot(lhs_loc, p[0:window_tokens], preferred_element_type=F32)
               + jnp.dot(lhs_ctx, p[window_tokens:], preferred_element_type=F32))
        ot_scr[hp, gi, rows, :] = o_t[0:C_HD] / o_t[C_HD:C_HD + 1]

    def half_step(u, read_base):
        score_stage(u + 2, 2 - read_base)
        score_stage(u + 3, 3 - read_base)
        value_stage(u, read_base)
        value_stage(u + 1, read_base + 1)

    def pipeline_step(j, carry):
        half_step(4 * j, 0)
        half_step(4 * j + 2, 2)
        return carry

    score_stage(jnp.int32(0), 0)
    score_stage(jnp.int32(1), 1)
    n_steps = (n_units - 4) // 4
    lax.fori_loop(0, n_steps, pipeline_step, 0)
    half_step(jnp.int32(n_units - 4), 0)
    value_stage(jnp.int32(n_units - 2), 2)
    value_stage(jnp.int32(n_units - 1), 3)

    gate = mod_ref[0, :, 2 * D_MODEL:3 * D_MODEL]
    for gi in range(groups_per_block):
        rows = slice(gi * group_tokens, (gi + 1) * group_tokens)
        attn = jnp.concatenate([ot_scr[t, gi].T for t in range(N_HEAD_PAIRS)], axis=1)
        yc = (attn * g_ref[rows, :].astype(F32)).astype(BF16)
        y = jnp.concatenate([yab_ref[rows, :], yc], axis=1)
        xo_ref[rows, :] = _post_norm(x_ref[rows, :], y, gate, w_out_ref, ln_g_ref, ln_b_ref)


def _lat_attn(layer, xs, yab, q, k, vt, gc, ck, cvt, tbl, mod, lw, seq_len):
    n_tok = xs.shape[0]
    n_batch = n_tok // seq_len
    blocks_per_seq = seq_len // TOKEN_BLOCK
    past = ck.shape[2]
    group_tokens = GROUP_ROWS * GRID_W
    tok_spec = lambda width: pl.BlockSpec((TOKEN_BLOCK, width),
                                          lambda b, r: (b * blocks_per_seq + r, 0))
    return pl.pallas_call(
        _lat_attn_kernel,
        grid=(n_batch, blocks_per_seq),
        in_specs=[
            tok_spec(D_MODEL),
            tok_spec(A_W + B_W),
            pl.BlockSpec((N_HEAD_PAIRS, TOKEN_BLOCK // group_tokens, LANES, group_tokens),
                         lambda b, r: (0, b * blocks_per_seq + r, 0, 0)),
            tok_spec(C_W),
            pl.BlockSpec((N_HEAD_PAIRS, seq_len, LANES), lambda b, r: (0, b, 0)),
            pl.BlockSpec((N_HEAD_PAIRS, seq_len // LANES, LANES, LANES), lambda b, r: (0, b, 0, 0)),
            pl.BlockSpec((N_HEAD_PAIRS, 1, past, LANES), lambda b, r: (0, b, 0, 0)),
            pl.BlockSpec((N_HEAD_PAIRS, 1, LANES, past), lambda b, r: (0, b, 0, 0)),
            _layer_spec(tbl.shape[1:], layer),
            pl.BlockSpec((1, 1, 3 * D_MODEL), lambda b, r: (b, 0, 0)),
            _layer_spec((D_MODEL, D_MODEL), layer),
            _const_spec((1, D_MODEL)),
            _const_spec((1, D_MODEL)),
        ],
        out_specs=tok_spec(D_MODEL),
        out_shape=jax.ShapeDtypeStruct((n_tok, D_MODEL), F32),
        scratch_shapes=[
            pltpu.VMEM((4, WINDOW_ROWS * GRID_W + past, group_tokens), F32),
            pltpu.VMEM((4, 1, group_tokens), F32),
            pltpu.VMEM((N_HEAD_PAIRS, TOKEN_BLOCK // group_tokens, LANES, group_tokens), F32),
        ],
        compiler_params=pltpu.CompilerParams(
            dimension_semantics=("arbitrary", "arbitrary"),
            vmem_limit_bytes=VMEM_LIMIT_BYTES),
        name=f"lat_attn_{layer}",
    )(xs, yab, q, gc, k, vt, ck, cvt, tbl, mod, lw["w_out"], lw["ln_g"], lw["ln_b"])


def _bias_table_kernel(rp_ref, o_ref):
    n_taps = 2 * WIN_COLS - 1
    n_dr = 2 * WIN_ROWS - 1
    cp = lax.broadcasted_iota(jnp.int32, (GRID_W, LANES), 0)
    lane = lax.broadcasted_iota(jnp.int32, (GRID_W, LANES), 1)
    c = lane & (GRID_W - 1)
    cs = jnp.clip(c - WIN_COLS // 2, 0, GRID_W - WIN_COLS)
    valid = (cp >= cs) & (cp < cs + WIN_COLS)
    low = lane < GRID_W
    neg = jnp.full((GRID_W, LANES), NEG_INF, F32)

    def diagonal(head, e, lane_offset):
        row = jnp.broadcast_to(rp_ref[0, head, e:e + 1, :], (GRID_W, LANES))
        shift = (LANES - (n_taps // 2) + lane_offset) % LANES
        return pltpu.roll(row, shift, 1, stride=1, stride_axis=0)

    def pair(head, e_low, e_high):
        lo = neg if e_low is None else diagonal(head, e_low, 0)
        hi = neg if e_high is None else diagonal(head, e_high, GRID_W)
        return jnp.where(valid, jnp.where(low, lo, hi) * LOG2_E, NEG_INF)

    for head in range(C_HEADS):
        for e in range(n_dr + 1):
            o_ref[0, head, e] = pair(head, e if e < n_dr else None, e - 1 if e >= 1 else None)
        o_ref[0, head, ENTRY_MASKED] = neg
        o_ref[0, head, ENTRY_LOW_ONLY] = pair(head, 3, None)
        o_ref[0, head, ENTRY_HIGH_ONLY] = pair(head, None, 10)


def _bias_tables(rpb):
    n_dr, n_taps = rpb.shape[2], rpb.shape[3]
    rp = jnp.pad(rpb[..., ::-1], ((0, 0), (0, 0), (0, 16 - n_dr), (0, LANES - n_taps)),
                 constant_values=NEG_INF)
    tbl = pl.pallas_call(
        _bias_table_kernel,
        grid=(DEPTH,),
        in_specs=[pl.BlockSpec((1, C_HEADS, 16, LANES), lambda l: (l, 0, 0, 0))],
        out_specs=pl.BlockSpec((1, C_HEADS, N_BIAS_ENTRIES, GRID_W, LANES), lambda l: (l, 0, 0, 0, 0)),
        out_shape=jax.ShapeDtypeStruct((DEPTH, C_HEADS, N_BIAS_ENTRIES, GRID_W, LANES), F32),
        compiler_params=pltpu.CompilerParams(dimension_semantics=("arbitrary",)),
        name="bias_tables",
    )(rp)
    return tbl.reshape(DEPTH, C_HEADS * N_BIAS_ENTRIES, GRID_W, LANES)


def _layer_weights(l, w_in_bf, w_out_bf, conv_w, gmlp_ln_g, gmlp_ln_b, ws_pairs, bs_lanes, ln_g, ln_b):
    return {
        "w_in": w_in_bf, "w_out": w_out_bf, "conv_w": conv_w[l],
        "gln_g": gmlp_ln_g[l][None], "gln_b": gmlp_ln_b[l][None],
        "ws": ws_pairs[l], "bs": bs_lanes[l],
        "ln_g": ln_g[l][None], "ln_b": ln_b[l][None],
    }


def kernel(x_prompt, x_sample, cache_k, cache_v, c, c_ctx, w_ada, b_ada, w_in, conv_w, gmlp_ln_g,
           gmlp_ln_b, w_spatial, b_spatial, rpb, w_out, ln_g, ln_b):
    batch, seq, d = x_prompt.shape
    dec_batch, dec_seq, _ = x_sample.shape
    past = cache_k.shape[2]

    n_mod_rows = 8
    cvec = jnp.concatenate(
        [c_ctx[None], c, jnp.zeros((n_mod_rows - 1 - dec_batch, d), F32)], axis=0)
    mod = _modulation(cvec, w_ada, b_ada)

    w_in_bf = w_in.astype(BF16)
    w_out_bf = w_out.astype(BF16)
    ws_pairs = w_spatial.reshape(DEPTH, 2, 2, CHUNK, CHUNK).transpose(0, 1, 3, 2, 4)
    ws_pairs = ws_pairs.reshape(DEPTH, 2, CHUNK, 2 * CHUNK).astype(BF16)
    bs_lanes = jnp.repeat(b_spatial.reshape(DEPTH, 2, 2, CHUNK).transpose(0, 1, 3, 2), C_HD, axis=-1)

    def pair_major(cache):
        cache = cache.reshape(dec_batch, DEPTH, past, N_HEAD_PAIRS, LANES)
        return cache.transpose(1, 3, 0, 2, 4).astype(BF16)

    ck_all = pair_major(cache_k)
    cvt_all = jnp.swapaxes(pair_major(cache_v), -1, -2)
    tables = _bias_tables(rpb)

    xp = x_prompt.reshape(batch * seq, d)
    xs = x_sample.reshape(dec_batch * dec_seq, d)
    kv = None
    for l in range(DEPTH):
        lw = _layer_weights(l, w_in_bf, w_out_bf, conv_w, gmlp_ln_g, gmlp_ln_b, ws_pairs, bs_lanes,
                            ln_g, ln_b)
        xp, new_k, new_v = _ctx_layer(l, xp, mod[l, 0:1][None], lw, kv)
        kv = (new_k, new_v)
        mod_lat = mod[l, 1:1 + dec_batch][:, None]
        yab, q, k, vt, gc = _lat_proj(l, xs, mod_lat, lw, dec_seq)
        xs = _lat_attn(l, xs, yab, q, k, vt, gc, ck_all[l], cvt_all[l], tables, mod_lat, lw, dec_seq)

    new_k, new_v = kv
    return (xp.reshape(batch, seq, d), xs.reshape(dec_batch, dec_seq, d),
            new_k.reshape(batch, DEPTH, seq, C_HEADS, C_HD),
            new_v.reshape(batch, DEPTH, seq, C_HEADS, C_HD))
```

```python
import functools

import jax
import jax.numpy as jnp
import numpy as np
from jax import lax
from jax.experimental import pallas as pl
from jax.experimental.pallas import tpu as pltpu

F32 = jnp.float32
BF16 = jnp.bfloat16

D_MODEL = 1024
DEPTH = 4
GRID_W = 64
A_W = 256
B_W = 256
CHUNK = 128
C_W = 512
C_HEADS = 8
C_HD = 64
WIN_ROWS = 8
WIN_COLS = 16
IN_W = 3840
ALPHA = (2 * DEPTH) ** 0.25
ATTN_SCALE = C_HD ** -0.5
NEG_INF = -1e30
LN_EPS = 1e-5
LOG2_E = 1.4426950408889634

LANES = 128
N_HEAD_PAIRS = C_W // LANES
A_COLS = (0, 4 * A_W)
B_COLS = (4 * A_W, 4 * A_W + 3 * B_W)
C_COLS = (4 * A_W + 3 * B_W, IN_W)

TOKEN_BLOCK = 512
ROW_PARTS = 2
SCORE_LEAD = 8
GROUP_ROWS = 4
WINDOW_ROWS = 12
ENTRY_MASKED = 16
ENTRY_LOW_ONLY = 17
ENTRY_HIGH_ONLY = 18
N_BIAS_ENTRIES = 19
VMEM_LIMIT_BYTES = 56 * 1024 * 1024


def _layernorm(x):
    mu = jnp.mean(x, axis=-1, keepdims=True)
    xc = x - mu
    var = jnp.mean(xc * xc, axis=-1, keepdims=True)
    return xc * lax.rsqrt(var + LN_EPS)


def _silu(x):
    return x / (1.0 + jnp.exp(-x))


def _lane_is_low(shape):
    return lax.broadcasted_iota(jnp.int32, shape, len(shape) - 1) < C_HD


def _mod_kernel(c_ref, w_ref, b_ref, o_ref):
    s = _silu(c_ref[...])
    o_ref[0] = jnp.dot(s, w_ref[0], precision=lax.Precision.HIGHEST,
                       preferred_element_type=F32) + b_ref[0]


def _modulation(cvec, w_ada, b_ada):
    n_rows = cvec.shape[0]
    col_block = 1024
    return pl.pallas_call(
        _mod_kernel,
        grid=(DEPTH, 3 * D_MODEL // col_block),
        in_specs=[
            pl.BlockSpec((n_rows, D_MODEL), lambda l, j: (0, 0)),
            pl.BlockSpec((1, D_MODEL, col_block), lambda l, j: (l, 0, j)),
            pl.BlockSpec((1, 1, col_block), lambda l, j: (l, 0, j)),
        ],
        out_specs=pl.BlockSpec((1, n_rows, col_block), lambda l, j: (l, 0, j)),
        out_shape=jax.ShapeDtypeStruct((DEPTH, n_rows, 3 * D_MODEL), F32),
        compiler_params=pltpu.CompilerParams(
            dimension_semantics=("arbitrary", "arbitrary"), vmem_limit_bytes=VMEM_LIMIT_BYTES),
        name="adaln_modulation",
    )(cvec, w_ada, b_ada.reshape(DEPTH, 1, 3 * D_MODEL))


def _modulated_norm(x, mod_ref):
    shift = mod_ref[0, :, 0:D_MODEL]
    scale = mod_ref[0, :, D_MODEL:2 * D_MODEL]
    return (_layernorm(x) * (1.0 + scale) + shift).astype(BF16)


def _modulated_norm_parts(x_ref, mod_ref):
    rows = x_ref.shape[0] // ROW_PARTS
    return [_modulated_norm(x_ref[i * rows:(i + 1) * rows, :], mod_ref) for i in range(ROW_PARTS)]


def _project(h_parts, w_in_ref, cols):
    return jnp.concatenate(
        [jnp.dot(h, w_in_ref[:, cols[0]:cols[1]], preferred_element_type=F32) for h in h_parts],
        axis=0)


def _conv_halo(h_halo, w_in_ref):
    xh = jnp.dot(h_halo, w_in_ref[:, 0:A_W], preferred_element_type=F32)
    ch = jnp.dot(h_halo, w_in_ref[:, 2 * A_W:3 * A_W], preferred_element_type=F32)
    return ch * xh


def _conv_branch(pa, z_halo, pos, seq_len, conv_w_ref):
    tb = pa.shape[0]
    xa, ba = pa[:, 0:A_W], pa[:, A_W:2 * A_W]
    ca, ga = pa[:, 2 * A_W:3 * A_W], pa[:, 3 * A_W:4 * A_W]
    z = ca * xa
    z_prev = pltpu.roll(z, 1, 0)
    z_next = pltpu.roll(z, tb - 1, 0)
    if z_halo is not None:
        row = lax.broadcasted_iota(jnp.int32, (tb, 1), 0)
        z_prev = jnp.where(row == 0, z_halo[7:8, :], z_prev)
        z_next = jnp.where(row == tb - 1, z_halo[8:9, :], z_next)
    z_prev = jnp.where(pos == 0, 0.0, z_prev)
    z_next = jnp.where(pos == seq_len - 1, 0.0, z_next)
    cw = conv_w_ref[...]
    conv = cw[0:1, :] * z_prev + cw[1:2, :] * z + cw[2:3, :] * z_next
    return ba * conv * _silu(ga)


def _gmlp_branch(pb, gln_g_ref, gln_b_ref, ws_ref, bs_ref):
    tb = pb.shape[0]
    u, v, gb = pb[:, 0:B_W], pb[:, B_W:2 * B_W], pb[:, 2 * B_W:3 * B_W]
    vn = _layernorm(v) * gln_g_ref[...] + gln_b_ref[...]
    low = _lane_is_low((CHUNK, LANES))
    chunks = []
    for c in range(tb // CHUNK):
        tiles = []
        for t in range(B_W // LANES):
            vt = vn[c * CHUNK:(c + 1) * CHUNK, t * LANES:(t + 1) * LANES]
            rhs = jnp.concatenate([jnp.where(low, vt, 0.0), jnp.where(low, 0.0, vt)],
                                  axis=0).astype(BF16)
            tiles.append(jnp.dot(ws_ref[t], rhs, preferred_element_type=F32) + bs_ref[t])
        chunks.append(jnp.concatenate(tiles, axis=1))
    sv = jnp.concatenate(chunks, axis=0)
    return u * sv * _silu(gb)


def _post_norm(x, y_bf16, gate, w_out_ref, ln_g_ref, ln_b_ref):
    out = jnp.dot(y_bf16, w_out_ref[...], preferred_element_type=F32)
    return _layernorm(ALPHA * x + gate * out) * ln_g_ref[...] + ln_b_ref[...]


def _ctx_layer_kernel(seq_len, x_ref, mod_ref, w_in_ref, conv_w_ref, gln_g_ref, gln_b_ref, ws_ref,
                      bs_ref, w_out_ref, ln_g_ref, ln_b_ref, *rest):
    xo_ref, ko_ref, vo_ref = rest[-3:]
    tb = x_ref.shape[0]
    n_seq = tb // seq_len
    h = _modulated_norm_parts(x_ref, mod_ref)
    pos = lax.broadcasted_iota(jnp.int32, (tb, 1), 0) & (seq_len - 1)
    pb = _project(h, w_in_ref, B_COLS)
    pa = _project(h, w_in_ref, A_COLS)
    yb = _gmlp_branch(pb, gln_g_ref, gln_b_ref, ws_ref, bs_ref)
    pc = _project(h, w_in_ref, C_COLS)
    ya = _conv_branch(pa, None, pos, seq_len, conv_w_ref)

    q, k = pc[:, 0:C_W], pc[:, C_W:2 * C_W]
    v, gc = pc[:, 2 * C_W:3 * C_W], pc[:, 3 * C_W:4 * C_W]
    qs = q * (ATTN_SCALE * LOG2_E)
    gate = mod_ref[0, :, 2 * D_MODEL:3 * D_MODEL]
    row_half = lax.shift_right_logical(lax.broadcasted_iota(jnp.int32, (LANES, seq_len), 0), 6)
    ones_rows = jnp.ones((16, seq_len), BF16)

    for s in range(n_seq):
        rows = slice(s * seq_len, (s + 1) * seq_len)
        ko_ref[s, 0] = k[rows, :]
        vo_ref[s, 0] = v[rows, :]
        def scores(head):
            lanes = slice((head // 2) * LANES, (head // 2 + 1) * LANES)
            k_t = k[rows, lanes].astype(BF16)
            qt = qs[rows, lanes].T.astype(BF16)
            qt_h = jnp.where(row_half == head % 2, qt, jnp.zeros_like(qt))
            return jnp.dot(k_t, qt_h, preferred_element_type=F32)

        def values(head, sc):
            lanes = slice((head // 2) * LANES, (head // 2 + 1) * LANES)
            vt = v[rows, lanes].T.astype(BF16)
            p = jnp.exp2(sc - jnp.max(sc, axis=0, keepdims=True)).astype(BF16)
            lhs = jnp.concatenate([vt[(head % 2) * C_HD:(head % 2 + 1) * C_HD, :], ones_rows], axis=0)
            o_t = jnp.dot(lhs, p, preferred_element_type=F32)
            return o_t[0:C_HD] / o_t[C_HD:C_HD + 1]

        pending, head_out = [], []
        for head in range(C_HEADS + SCORE_LEAD):
            if head < C_HEADS:
                pending.append(scores(head))
            if head >= SCORE_LEAD:
                head_out.append(values(head - SCORE_LEAD, pending[head - SCORE_LEAD]))
        attn = jnp.concatenate(
            [jnp.concatenate(head_out[2 * t:2 * t + 2], axis=0).T for t in range(N_HEAD_PAIRS)],
            axis=1)
        yc = attn * _silu(gc[rows, :])
        y = jnp.concatenate([ya[rows, :], yb[rows, :], yc], axis=1).astype(BF16)
        xo_ref[rows, :] = _post_norm(x_ref[rows, :], y, gate, w_out_ref, ln_g_ref, ln_b_ref)


def _const_spec(shape):
    return pl.BlockSpec(shape, lambda *_: (0,) * len(shape))


def _layer_spec(shape, layer):
    return pl.BlockSpec((None, *shape), lambda *_: (layer,) + (0,) * len(shape))


def _weight_specs(layer):
    return [
        _layer_spec((D_MODEL, IN_W), layer),
        _const_spec((3, A_W)),
        _const_spec((1, B_W)),
        _const_spec((1, B_W)),
        _const_spec((B_W // LANES, CHUNK, 2 * CHUNK)),
        _const_spec((B_W // LANES, CHUNK, LANES)),
    ]


def _ctx_layer(layer, xp, mod, lw, kv_prev):
    n_tok = xp.shape[0]
    seq_len = 256
    n_seq_total = n_tok // seq_len
    seq_per_block = TOKEN_BLOCK // seq_len
    kv_shape = jax.ShapeDtypeStruct((n_seq_total, DEPTH, seq_len, C_W), F32)
    kv_spec = pl.BlockSpec((seq_per_block, 1, seq_len, C_W), lambda i: (i, layer, 0, 0))
    in_specs = [
        pl.BlockSpec((TOKEN_BLOCK, D_MODEL), lambda i: (i, 0)),
        _const_spec((1, 1, 3 * D_MODEL)),
        *_weight_specs(layer),
        _layer_spec((D_MODEL, D_MODEL), layer),
        _const_spec((1, D_MODEL)),
        _const_spec((1, D_MODEL)),
    ]
    args = [xp, mod, lw["w_in"], lw["conv_w"], lw["gln_g"], lw["gln_b"], lw["ws"], lw["bs"],
            lw["w_out"], lw["ln_g"], lw["ln_b"]]
    aliases = {}
    if kv_prev is not None:
        in_specs += [pl.BlockSpec(memory_space=pl.ANY), pl.BlockSpec(memory_space=pl.ANY)]
        aliases = {len(args): 1, len(args) + 1: 2}
        args += list(kv_prev)
    return pl.pallas_call(
        functools.partial(_ctx_layer_kernel, seq_len),
        grid=(n_tok // TOKEN_BLOCK,),
        in_specs=in_specs,
        out_specs=[pl.BlockSpec((TOKEN_BLOCK, D_MODEL), lambda i: (i, 0)), kv_spec, kv_spec],
        out_shape=[jax.ShapeDtypeStruct((n_tok, D_MODEL), F32), kv_shape, kv_shape],
        input_output_aliases=aliases,
        compiler_params=pltpu.CompilerParams(
            dimension_semantics=("arbitrary",), vmem_limit_bytes=VMEM_LIMIT_BYTES),
        name=f"ctx_layer_{layer}",
    )(*args)


def _lat_proj_kernel(seq_len, x_ref, xprev_ref, xnext_ref, mod_ref, w_in_ref, conv_w_ref, gln_g_ref,
                     gln_b_ref, ws_ref, bs_ref, yab_ref, qt_ref, k_ref, vt_ref, g_ref):
    tb = x_ref.shape[0]
    group_tokens = GROUP_ROWS * GRID_W
    h = _modulated_norm_parts(x_ref, mod_ref)
    h_halo = _modulated_norm(jnp.concatenate([xprev_ref[0], xnext_ref[0]], axis=0), mod_ref)
    pos = (pl.program_id(0) * tb + lax.broadcasted_iota(jnp.int32, (tb, 1), 0)) & (seq_len - 1)
    z_halo = _conv_halo(h_halo, w_in_ref)
    pb = _project(h, w_in_ref, B_COLS)
    pa = _project(h, w_in_ref, A_COLS)
    yb = _gmlp_branch(pb, gln_g_ref, gln_b_ref, ws_ref, bs_ref)
    pvg = _project(h, w_in_ref, (C_COLS[0] + 2 * C_W, C_COLS[1]))
    ya = _conv_branch(pa, z_halo, pos, seq_len, conv_w_ref)
    yab_ref[...] = jnp.concatenate([ya, yb], axis=1).astype(BF16)
    pqk = _project(h, w_in_ref, (C_COLS[0], C_COLS[0] + 2 * C_W))
    v = pvg[:, 0:C_W]
    for t in range(N_HEAD_PAIRS):
        for n in range(tb // LANES):
            vt_ref[t, n] = v[n * LANES:(n + 1) * LANES, t * LANES:(t + 1) * LANES].T.astype(BF16)
    g_ref[...] = _silu(pvg[:, C_W:2 * C_W]).astype(BF16)
    qs = pqk[:, 0:C_W] * (ATTN_SCALE * LOG2_E)
    k = pqk[:, C_W:2 * C_W]
    for t in range(N_HEAD_PAIRS):
        k_ref[t] = k[:, t * LANES:(t + 1) * LANES].astype(BF16)
        for n in range(tb // group_tokens):
            qt_ref[t, n] = qs[n * group_tokens:(n + 1) * group_tokens,
                              t * LANES:(t + 1) * LANES].T.astype(BF16)


def _lat_proj(layer, xs, mod, lw, seq_len):
    n_tok = xs.shape[0]
    blocks_per_seq = seq_len // TOKEN_BLOCK
    n_halo = n_tok // 8
    halo_per_block = TOKEN_BLOCK // 8
    xs8 = xs.reshape(n_halo, 8, D_MODEL)
    group_tokens = GROUP_ROWS * GRID_W
    pair_shape = jax.ShapeDtypeStruct((N_HEAD_PAIRS, n_tok, LANES), BF16)
    pair_spec = pl.BlockSpec((N_HEAD_PAIRS, TOKEN_BLOCK, LANES), lambda i: (0, i, 0))
    return pl.pallas_call(
        functools.partial(_lat_proj_kernel, seq_len),
        grid=(n_tok // TOKEN_BLOCK,),
        in_specs=[
            pl.BlockSpec((TOKEN_BLOCK, D_MODEL), lambda i: (i, 0)),
            pl.BlockSpec((1, 8, D_MODEL), lambda i: (jnp.maximum(i * halo_per_block - 1, 0), 0, 0)),
            pl.BlockSpec((1, 8, D_MODEL),
                         lambda i: (jnp.minimum((i + 1) * halo_per_block, n_halo - 1), 0, 0)),
            pl.BlockSpec((1, 1, 3 * D_MODEL), lambda i: (i // blocks_per_seq, 0, 0)),
            *_weight_specs(layer),
        ],
        out_specs=[
            pl.BlockSpec((TOKEN_BLOCK, A_W + B_W), lambda i: (i, 0)),
            pl.BlockSpec((N_HEAD_PAIRS, TOKEN_BLOCK // group_tokens, LANES, group_tokens),
                         lambda i: (0, i, 0, 0)),
            pair_spec,
            pl.BlockSpec((N_HEAD_PAIRS, TOKEN_BLOCK // LANES, LANES, LANES), lambda i: (0, i, 0, 0)),
            pl.BlockSpec((TOKEN_BLOCK, C_W), lambda i: (i, 0)),
        ],
        out_shape=[
            jax.ShapeDtypeStruct((n_tok, A_W + B_W), BF16),
            jax.ShapeDtypeStruct((N_HEAD_PAIRS, n_tok // group_tokens, LANES, group_tokens), BF16),
            pair_shape,
            jax.ShapeDtypeStruct((N_HEAD_PAIRS, n_tok // LANES, LANES, LANES), BF16),
            jax.ShapeDtypeStruct((n_tok, C_W), BF16),
        ],
        compiler_params=pltpu.CompilerParams(
            dimension_semantics=("arbitrary",), vmem_limit_bytes=VMEM_LIMIT_BYTES),
        name=f"lat_proj_{layer}",
    )(xs, xs8, xs8, mod, lw["w_in"], lw["conv_w"], lw["gln_g"], lw["gln_b"], lw["ws"], lw["bs"])


GROUP_INTERIOR, GROUP_FIRST, GROUP_LAST = 0, 1, 2


def _bias_entry_table():
    table = np.zeros((3, WINDOW_ROWS * (GROUP_ROWS // 2)), np.int32)
    for kind, base in ((GROUP_INTERIOR, 3), (GROUP_FIRST, 7), (GROUP_LAST, -1)):
        for j in range(WINDOW_ROWS):
            for ip in range(GROUP_ROWS // 2):
                e = base + j - 2 * ip
                if kind == GROUP_INTERIOR:
                    entry = (ENTRY_LOW_ONLY if e == 3 else ENTRY_HIGH_ONLY if e == 11
                             else ENTRY_MASKED if (e < 3 or e > 11) else e)
                elif kind == GROUP_FIRST:
                    entry = ENTRY_MASKED if j >= WIN_ROWS else e
                else:
                    entry = ENTRY_MASKED if j < WINDOW_ROWS - WIN_ROWS else e
                assert 0 <= entry < N_BIAS_ENTRIES
                table[kind, j * (GROUP_ROWS // 2) + ip] = entry
    return table


def _lat_attn_kernel(ent_ref, x_ref, yab_ref, qt_ref, g_ref, k_ref, vt_ref, ck_ref, cvt_ref, tbl_ref,
                     mod_ref, w_out_ref, ln_g_ref, ln_b_ref, xo_ref, s_scr, m_scr, ot_scr):
    rb = pl.program_id(1)
    tb = x_ref.shape[0]
    group_tokens = GROUP_ROWS * GRID_W
    window_tokens = WINDOW_ROWS * GRID_W
    window_tiles = window_tokens // LANES
    n_groups = (k_ref.shape[1] // GRID_W) // GROUP_ROWS
    groups_per_block = tb // group_tokens
    past = ck_ref.shape[2]
    n_units = N_HEAD_PAIRS * groups_per_block * 2

    def coords(u):
        hp = lax.shift_right_logical(u, 2)
        gi = lax.shift_right_logical(u, 1) & 1
        parity = u & 1
        g = rb * groups_per_block + gi
        first_tile = jnp.clip(2 * g - 2, 0, (GRID_W - WINDOW_ROWS) // 2)
        return hp, gi, parity, g, first_tile

    def score_stage(u, slot):
        hp, gi, parity, g, first_tile = coords(u)
        start = pl.multiple_of(first_tile * LANES, LANES)
        kw = k_ref[hp, pl.ds(start, window_tokens), :]
        qt = qt_ref[hp, gi]
        row_half = lax.shift_right_logical(
            lax.broadcasted_iota(jnp.int32, (LANES, group_tokens), 0), 6)
        qt_h = jnp.where(row_half == parity, qt, jnp.zeros_like(qt))
        half = (window_tokens + past) // 2
        s_a = jnp.dot(kw[0:half], qt_h, preferred_element_type=F32)
        s_b = jnp.dot(jnp.concatenate([kw[half:], ck_ref[hp, 0]], axis=0), qt_h,
                      preferred_element_type=F32)
        s_loc = jnp.concatenate([s_a, s_b[0:window_tokens - half]], axis=0)
        s_ctx = s_b[window_tokens - half:]
        head_base = (2 * hp + parity) * N_BIAS_ENTRIES
        kind = jnp.where(g == 0, GROUP_FIRST, jnp.where(g == n_groups - 1, GROUP_LAST, GROUP_INTERIOR))
        bias_rows = []
        for j in range(WINDOW_ROWS):
            tiles = []
            for ip in range(GROUP_ROWS // 2):
                tiles.append(tbl_ref[head_base + ent_ref[kind, j * (GROUP_ROWS // 2) + ip]])
            bias_rows.append(jnp.concatenate(tiles, axis=1))
        s_loc = s_loc + jnp.concatenate(bias_rows, axis=0)
        s_scr[slot, 0:window_tokens, :] = s_loc
        s_scr[slot, window_tokens:, :] = s_ctx
        m_scr[slot] = jnp.maximum(jnp.max(s_loc, axis=0, keepdims=True),
                                  jnp.max(s_ctx, axis=0, keepdims=True))

    def value_stage(u, slot):
        hp, gi, parity, g, first_tile = coords(u)
        p = jnp.exp2(s_scr[slot] - m_scr[slot]).astype(BF16)
        rows = pl.ds(pl.multiple_of(parity * C_HD, C_HD), C_HD)
        vt_tiles = vt_ref[hp, pl.ds(first_tile, window_tiles), rows, :]
        vtw = jnp.concatenate([vt_tiles[n] for n in range(window_tiles)], axis=1)
        ones_rows = 16
        lhs_loc = jnp.concatenate([vtw, jnp.ones((ones_rows, window_tokens), BF16)], axis=0)
        lhs_ctx = jnp.concatenate(
            [cvt_ref[hp, 0, rows, :], jnp.ones((ones_rows, s_scr.shape[1] - window_tokens), BF16)],
            axis=0)
        o_t = (jnp.dot(lhs_loc, p[0:window_tokens], preferred_element_type=F32)
               + jnp.dot(lhs_ctx, p[window_tokens:], preferred_element_type=F32))
        ot_scr[hp, gi, rows, :] = o_t[0:C_HD] / o_t[C_HD:C_HD + 1]

    def half_step(u, read_base):
        score_stage(u + 2, 2 - read_base)
        score_stage(u + 3, 3 - read_base)
        value_stage(u, read_base)
        value_stage(u + 1, read_base + 1)

    def pipeline_step(j, carry):
        half_step(4 * j, 0)
        half_step(4 * j + 2, 2)
        return carry

    score_stage(jnp.int32(0), 0)
    score_stage(jnp.int32(1), 1)
    n_steps = (n_units - 4) // 4
    lax.fori_loop(0, n_steps, pipeline_step, 0)
    half_step(jnp.int32(n_units - 4), 0)
    value_stage(jnp.int32(n_units - 2), 2)
    value_stage(jnp.int32(n_units - 1), 3)

    gate = mod_ref[0, :, 2 * D_MODEL:3 * D_MODEL]
    for gi in range(groups_per_block):
        rows = slice(gi * group_tokens, (gi + 1) * group_tokens)
        attn = jnp.concatenate([ot_scr[t, gi].T for t in range(N_HEAD_PAIRS)], axis=1)
        yc = (attn * g_ref[rows, :].astype(F32)).astype(BF16)
        y = jnp.concatenate([yab_ref[rows, :], yc], axis=1)
        xo_ref[rows, :] = _post_norm(x_ref[rows, :], y, gate, w_out_ref, ln_g_ref, ln_b_ref)


def _lat_attn(layer, xs, yab, q, k, vt, gc, ck, cvt, tbl, mod, lw, seq_len):
    n_tok = xs.shape[0]
    n_batch = n_tok // seq_len
    blocks_per_seq = seq_len // TOKEN_BLOCK
    past = ck.shape[2]
    group_tokens = GROUP_ROWS * GRID_W
    tok_spec = lambda width: pl.BlockSpec((TOKEN_BLOCK, width),
                                          lambda b, r: (b * blocks_per_seq + r, 0))
    return pl.pallas_call(
        _lat_attn_kernel,
        grid=(n_batch, blocks_per_seq),
        in_specs=[
            pl.BlockSpec(memory_space=pltpu.SMEM),
            tok_spec(D_MODEL),
            tok_spec(A_W + B_W),
            pl.BlockSpec((N_HEAD_PAIRS, TOKEN_BLOCK // group_tokens, LANES, group_tokens),
                         lambda b, r: (0, b * blocks_per_seq + r, 0, 0)),
            tok_spec(C_W),
            pl.BlockSpec((N_HEAD_PAIRS, seq_len, LANES), lambda b, r: (0, b, 0)),
            pl.BlockSpec((N_HEAD_PAIRS, seq_len // LANES, LANES, LANES), lambda b, r: (0, b, 0, 0)),
            pl.BlockSpec((N_HEAD_PAIRS, 1, past, LANES), lambda b, r: (0, b, 0, 0)),
            pl.BlockSpec((N_HEAD_PAIRS, 1, LANES, past), lambda b, r: (0, b, 0, 0)),
            _layer_spec(tbl.shape[1:], layer),
            pl.BlockSpec((1, 1, 3 * D_MODEL), lambda b, r: (b, 0, 0)),
            _layer_spec((D_MODEL, D_MODEL), layer),
            _const_spec((1, D_MODEL)),
            _const_spec((1, D_MODEL)),
        ],
        out_specs=tok_spec(D_MODEL),
        out_shape=jax.ShapeDtypeStruct((n_tok, D_MODEL), F32),
        scratch_shapes=[
            pltpu.VMEM((4, WINDOW_ROWS * GRID_W + past, group_tokens), F32),
            pltpu.VMEM((4, 1, group_tokens), F32),
            pltpu.VMEM((N_HEAD_PAIRS, TOKEN_BLOCK // group_tokens, LANES, group_tokens), F32),
        ],
        compiler_params=pltpu.CompilerParams(
            dimension_semantics=("arbitrary", "arbitrary"),
            vmem_limit_bytes=VMEM_LIMIT_BYTES),
        name=f"lat_attn_{layer}",
    )(jnp.asarray(_bias_entry_table()), xs, yab, q, gc, k, vt, ck, cvt, tbl, mod, lw["w_out"],
      lw["ln_g"], lw["ln_b"])


def _bias_table_kernel(rp_ref, o_ref):
    n_taps = 2 * WIN_COLS - 1
    n_dr = 2 * WIN_ROWS - 1
    cp = lax.broadcasted_iota(jnp.int32, (GRID_W, LANES), 0)
    lane = lax.broadcasted_iota(jnp.int32, (GRID_W, LANES), 1)
    c = lane & (GRID_W - 1)
    cs = jnp.clip(c - WIN_COLS // 2, 0, GRID_W - WIN_COLS)
    valid = (cp >= cs) & (cp < cs + WIN_COLS)
    low = lane < GRID_W
    neg = jnp.full((GRID_W, LANES), NEG_INF, F32)

    def diagonal(head, e, lane_offset):
        row = jnp.broadcast_to(rp_ref[0, head, e:e + 1, :], (GRID_W, LANES))
        shift = (LANES - (n_taps // 2) + lane_offset) % LANES
        return pltpu.roll(row, shift, 1, stride=1, stride_axis=0)

    def pair(head, e_low, e_high):
        lo = neg if e_low is None else diagonal(head, e_low, 0)
        hi = neg if e_high is None else diagonal(head, e_high, GRID_W)
        return jnp.where(valid, jnp.where(low, lo, hi) * LOG2_E, NEG_INF)

    for head in range(C_HEADS):
        for e in range(n_dr + 1):
            o_ref[0, head, e] = pair(head, e if e < n_dr else None, e - 1 if e >= 1 else None)
        o_ref[0, head, ENTRY_MASKED] = neg
        o_ref[0, head, ENTRY_LOW_ONLY] = pair(head, 3, None)
        o_ref[0, head, ENTRY_HIGH_ONLY] = pair(head, None, 10)


def _bias_tables(rpb):
    n_dr, n_taps = rpb.shape[2], rpb.shape[3]
    rp = jnp.pad(rpb[..., ::-1], ((0, 0), (0, 0), (0, 16 - n_dr), (0, LANES - n_taps)),
                 constant_values=NEG_INF)
    tbl = pl.pallas_call(
        _bias_table_kernel,
        grid=(DEPTH,),
        in_specs=[pl.BlockSpec((1, C_HEADS, 16, LANES), lambda l: (l, 0, 0, 0))],
        out_specs=pl.BlockSpec((1, C_HEADS, N_BIAS_ENTRIES, GRID_W, LANES), lambda l: (l, 0, 0, 0, 0)),
        out_shape=jax.ShapeDtypeStruct((DEPTH, C_HEADS, N_BIAS_ENTRIES, GRID_W, LANES), F32),
        compiler_params=pltpu.CompilerParams(dimension_semantics=("arbitrary",)),
        name="bias_tables",
    )(rp)
    return tbl.reshape(DEPTH, C_HEADS * N_BIAS_ENTRIES, GRID_W, LANES)


def _layer_weights(l, w_in_bf, w_out_bf, conv_w, gmlp_ln_g, gmlp_ln_b, ws_pairs, bs_lanes, ln_g, ln_b):
    return {
        "w_in": w_in_bf, "w_out": w_out_bf, "conv_w": conv_w[l],
        "gln_g": gmlp_ln_g[l][None], "gln_b": gmlp_ln_b[l][None],
        "ws": ws_pairs[l], "bs": bs_lanes[l],
        "ln_g": ln_g[l][None], "ln_b": ln_b[l][None],
    }


def kernel(x_prompt, x_sample, cache_k, cache_v, c, c_ctx, w_ada, b_ada, w_in, conv_w, gmlp_ln_g,
           gmlp_ln_b, w_spatial, b_spatial, rpb, w_out, ln_g, ln_b):
    batch, seq, d = x_prompt.shape
    dec_batch, dec_seq, _ = x_sample.shape
    past = cache_k.shape[2]

    n_mod_rows = 8
    cvec = jnp.concatenate(
        [c_ctx[None], c, jnp.zeros((n_mod_rows - 1 - dec_batch, d), F32)], axis=0)
    mod = _modulation(cvec, w_ada, b_ada)

    w_in_bf = w_in.astype(BF16)
    w_out_bf = w_out.astype(BF16)
    ws_pairs = w_spatial.reshape(DEPTH, 2, 2, CHUNK, CHUNK).transpose(0, 1, 3, 2, 4)
    ws_pairs = ws_pairs.reshape(DEPTH, 2, CHUNK, 2 * CHUNK).astype(BF16)
    bs_lanes = jnp.repeat(b_spatial.reshape(DEPTH, 2, 2, CHUNK).transpose(0, 1, 3, 2), C_HD, axis=-1)

    def pair_major(cache):
        cache = cache.reshape(dec_batch, DEPTH, past, N_HEAD_PAIRS, LANES)
        return cache.transpose(1, 3, 0, 2, 4).astype(BF16)

    ck_all = pair_major(cache_k)
    cvt_all = jnp.swapaxes(pair_major(cache_v), -1, -2)
    tables = _bias_tables(rpb)

    xp = x_prompt.reshape(batch * seq, d)
    xs = x_sample.reshape(dec_batch * dec_seq, d)
    kv = None
    for l in range(DEPTH):
        lw = _layer_weights(l, w_in_bf, w_out_bf, conv_w, gmlp_ln_g, gmlp_ln_b, ws_pairs, bs_lanes,
                            ln_g, ln_b)
        xp, new_k, new_v = _ctx_layer(l, xp, mod[l, 0:1][None], lw, kv)
        kv = (new_k, new_v)
        mod_lat = mod[l, 1:1 + dec_batch][:, None]
        yab, q, k, vt, gc = _lat_proj(l, xs, mod_lat, lw, dec_seq)
        xs = _lat_attn(l, xs, yab, q, k, vt, gc, ck_all[l], cvt_all[l], tables, mod_lat, lw, dec_seq)

    new_k, new_v = kv
    return (xp.reshape(batch, seq, d), xs.reshape(dec_batch, dec_seq, d),
            new_k.reshape(batch, DEPTH, seq, C_HEADS, C_HD),
            new_v.reshape(batch, DEPTH, seq, C_HEADS, C_HD))
```

```python
import functools

import jax
import jax.numpy as jnp
import numpy as np
from jax import lax
from jax.experimental import pallas as pl
from jax.experimental.pallas import tpu as pltpu

F32 = jnp.float32
BF16 = jnp.bfloat16

D_MODEL = 1024
DEPTH = 4
GRID_W = 64
A_W = 256
B_W = 256
CHUNK = 128
C_W = 512
C_HEADS = 8
C_HD = 64
WIN_ROWS = 8
WIN_COLS = 16
IN_W = 3840
ALPHA = (2 * DEPTH) ** 0.25
ATTN_SCALE = C_HD ** -0.5
NEG_INF = -1e30
LN_EPS = 1e-5
LOG2_E = 1.4426950408889634

LANES = 128
N_HEAD_PAIRS = C_W // LANES
A_COLS = (0, 4 * A_W)
B_COLS = (4 * A_W, 4 * A_W + 3 * B_W)
C_COLS = (4 * A_W + 3 * B_W, IN_W)

TOKEN_BLOCK = 512
ATTN_TOKEN_BLOCK = 1024
FIRST_SLAB_ROWS = 256
SCORE_LEAD = 8
GROUP_ROWS = 4
WINDOW_ROWS = 12
ENTRY_MASKED = 16
ENTRY_LOW_ONLY = 17
ENTRY_HIGH_ONLY = 18
N_BIAS_ENTRIES = 19
VMEM_LIMIT_BYTES = 56 * 1024 * 1024


def _layernorm(x):
    mu = jnp.mean(x, axis=-1, keepdims=True)
    xc = x - mu
    var = jnp.mean(xc * xc, axis=-1, keepdims=True)
    return xc * lax.rsqrt(var + LN_EPS)


def _silu(x):
    return x / (1.0 + jnp.exp(-x))


def _lane_is_low(shape):
    return lax.broadcasted_iota(jnp.int32, shape, len(shape) - 1) < C_HD


def _mod_kernel(c_ref, w_ref, b_ref, o_ref):
    s = _silu(c_ref[...])
    w = w_ref[0]
    s_hi, w_hi = s.astype(BF16), w.astype(BF16)
    s_lo = (s - s_hi.astype(F32)).astype(BF16)
    w_lo = (w - w_hi.astype(F32)).astype(BF16)
    o_ref[0] = (jnp.dot(s_hi, w_hi, preferred_element_type=F32)
                + (jnp.dot(s_hi, w_lo, preferred_element_type=F32)
                   + jnp.dot(s_lo, w_hi, preferred_element_type=F32))) + b_ref[0]


def _modulation(cvec, w_ada, b_ada):
    n_rows = cvec.shape[0]
    col_block = 1024
    return pl.pallas_call(
        _mod_kernel,
        grid=(DEPTH, 3 * D_MODEL // col_block),
        in_specs=[
            pl.BlockSpec((n_rows, D_MODEL), lambda l, j: (0, 0)),
            pl.BlockSpec((1, D_MODEL, col_block), lambda l, j: (l, 0, j)),
            pl.BlockSpec((1, 1, col_block), lambda l, j: (l, 0, j)),
        ],
        out_specs=pl.BlockSpec((1, n_rows, col_block), lambda l, j: (l, 0, j)),
        out_shape=jax.ShapeDtypeStruct((DEPTH, n_rows, 3 * D_MODEL), F32),
        compiler_params=pltpu.CompilerParams(
            dimension_semantics=("arbitrary", "arbitrary"), vmem_limit_bytes=VMEM_LIMIT_BYTES),
        name="adaln_modulation",
    )(cvec, w_ada, b_ada.reshape(DEPTH, 1, 3 * D_MODEL))


def _modulated_norm(x, mod_ref):
    shift = mod_ref[0, :, 0:D_MODEL]
    scale = mod_ref[0, :, D_MODEL:2 * D_MODEL]
    return (_layernorm(x) * (1.0 + scale) + shift).astype(BF16)


def _modulated_norm_parts(x_ref, mod_ref):
    bounds = (0, FIRST_SLAB_ROWS, x_ref.shape[0])
    return [_modulated_norm(x_ref[lo:hi, :], mod_ref) for lo, hi in zip(bounds[:-1], bounds[1:])]


def _project(h_parts, w_in_ref, cols):
    return jnp.concatenate(
        [jnp.dot(h, w_in_ref[:, cols[0]:cols[1]], preferred_element_type=F32) for h in h_parts],
        axis=0)


def _conv_halo(h_halo, w_in_ref):
    xh = jnp.dot(h_halo, w_in_ref[:, 0:A_W], preferred_element_type=F32)
    ch = jnp.dot(h_halo, w_in_ref[:, 2 * A_W:3 * A_W], preferred_element_type=F32)
    return ch * xh


def _conv_branch(pa, z_halo, pos, seq_len, conv_w_ref):
    tb = pa.shape[0]
    xa, ba = pa[:, 0:A_W], pa[:, A_W:2 * A_W]
    ca, ga = pa[:, 2 * A_W:3 * A_W], pa[:, 3 * A_W:4 * A_W]
    z = ca * xa
    z_prev = pltpu.roll(z, 1, 0)
    z_next = pltpu.roll(z, tb - 1, 0)
    if z_halo is not None:
        row = lax.broadcasted_iota(jnp.int32, (tb, 1), 0)
        z_prev = jnp.where(row == 0, z_halo[7:8, :], z_prev)
        z_next = jnp.where(row == tb - 1, z_halo[8:9, :], z_next)
    z_prev = jnp.where(pos == 0, 0.0, z_prev)
    z_next = jnp.where(pos == seq_len - 1, 0.0, z_next)
    cw = conv_w_ref[...]
    conv = cw[0:1, :] * z_prev + cw[1:2, :] * z + cw[2:3, :] * z_next
    return ba * conv * _silu(ga)


def _gmlp_branch(pb, gln_g_ref, gln_b_ref, ws_ref, bs_ref):
    tb = pb.shape[0]
    u, v, gb = pb[:, 0:B_W], pb[:, B_W:2 * B_W], pb[:, 2 * B_W:3 * B_W]
    vn = _layernorm(v) * gln_g_ref[...] + gln_b_ref[...]
    low = _lane_is_low((CHUNK, LANES))
    chunks = []
    for c in range(tb // CHUNK):
        tiles = []
        for t in range(B_W // LANES):
            vt = vn[c * CHUNK:(c + 1) * CHUNK, t * LANES:(t + 1) * LANES]
            rhs = jnp.concatenate([jnp.where(low, vt, 0.0), jnp.where(low, 0.0, vt)],
                                  axis=0).astype(BF16)
            tiles.append(jnp.dot(ws_ref[t], rhs, preferred_element_type=F32) + bs_ref[t])
        chunks.append(jnp.concatenate(tiles, axis=1))
    sv = jnp.concatenate(chunks, axis=0)
    return u * sv * _silu(gb)


def _post_norm(x, y_bf16, gate, w_out_ref, ln_g_ref, ln_b_ref):
    out = jnp.dot(y_bf16, w_out_ref[...], preferred_element_type=F32)
    return _layernorm(ALPHA * x + gate * out) * ln_g_ref[...] + ln_b_ref[...]


def _ctx_layer_kernel(seq_len, x_ref, mod_ref, w_in_ref, conv_w_ref, gln_g_ref, gln_b_ref, ws_ref,
                      bs_ref, w_out_ref, ln_g_ref, ln_b_ref, *rest):
    xo_ref, ko_ref, vo_ref = rest[-3:]
    tb = x_ref.shape[0]
    n_seq = tb // seq_len
    h = _modulated_norm_parts(x_ref, mod_ref)
    pos = lax.broadcasted_iota(jnp.int32, (tb, 1), 0) & (seq_len - 1)
    pb = _project(h, w_in_ref, B_COLS)
    pa = _project(h, w_in_ref, A_COLS)
    yb = _gmlp_branch(pb, gln_g_ref, gln_b_ref, ws_ref, bs_ref)
    pc = _project(h, w_in_ref, C_COLS)
    ya = _conv_branch(pa, None, pos, seq_len, conv_w_ref)

    q, k = pc[:, 0:C_W], pc[:, C_W:2 * C_W]
    v, gc = pc[:, 2 * C_W:3 * C_W], pc[:, 3 * C_W:4 * C_W]
    qs = q * (ATTN_SCALE * LOG2_E)
    gate = mod_ref[0, :, 2 * D_MODEL:3 * D_MODEL]
    row_half = lax.shift_right_logical(lax.broadcasted_iota(jnp.int32, (LANES, seq_len), 0), 6)
    ones_rows = jnp.ones((16, seq_len), BF16)

    for s in range(n_seq):
        rows = slice(s * seq_len, (s + 1) * seq_len)
        ko_ref[s, 0] = k[rows, :]
        vo_ref[s, 0] = v[rows, :]
        for later in range(1, ko_ref.shape[1]):
            ko_ref[s, later] = jnp.zeros((seq_len, C_W), F32)
            vo_ref[s, later] = jnp.zeros((seq_len, C_W), F32)
        def scores(head):
            lanes = slice((head // 2) * LANES, (head // 2 + 1) * LANES)
            k_t = k[rows, lanes].astype(BF16)
            qt = qs[rows, lanes].T.astype(BF16)
            qt_h = jnp.where(row_half == head % 2, qt, jnp.zeros_like(qt))
            return jnp.dot(k_t, qt_h, preferred_element_type=F32)

        def values(head, sc):
            lanes = slice((head // 2) * LANES, (head // 2 + 1) * LANES)
            vt = v[rows, lanes].T.astype(BF16)
            p = jnp.exp2(sc - jnp.max(sc, axis=0, keepdims=True)).astype(BF16)
            lhs = jnp.concatenate([vt[(head % 2) * C_HD:(head % 2 + 1) * C_HD, :], ones_rows], axis=0)
            o_t = jnp.dot(lhs, p, preferred_element_type=F32)
            return o_t[0:C_HD] / o_t[C_HD:C_HD + 1]

        pending, head_out = [], []
        for head in range(C_HEADS + SCORE_LEAD):
            if head < C_HEADS:
                pending.append(scores(head))
            if head >= SCORE_LEAD:
                head_out.append(values(head - SCORE_LEAD, pending[head - SCORE_LEAD]))
        attn = jnp.concatenate(
            [jnp.concatenate(head_out[2 * t:2 * t + 2], axis=0).T for t in range(N_HEAD_PAIRS)],
            axis=1)
        yc = attn * _silu(gc[rows, :])
        y = jnp.concatenate([ya[rows, :], yb[rows, :], yc], axis=1).astype(BF16)
        xo_ref[rows, :] = _post_norm(x_ref[rows, :], y, gate, w_out_ref, ln_g_ref, ln_b_ref)


def _const_spec(shape):
    return pl.BlockSpec(shape, lambda *_: (0,) * len(shape), pipeline_mode=pl.Buffered(1))


def _layer_spec(shape, layer):
    return pl.BlockSpec((None, *shape), lambda *_: (layer,) + (0,) * len(shape),
                        pipeline_mode=pl.Buffered(1))


def _weight_specs(layer):
    return [
        _layer_spec((D_MODEL, IN_W), layer),
        _const_spec((3, A_W)),
        _const_spec((1, B_W)),
        _const_spec((1, B_W)),
        _const_spec((B_W // LANES, CHUNK, 2 * CHUNK)),
        _const_spec((B_W // LANES, CHUNK, LANES)),
    ]


def _ctx_layer(layer, xp, mod, lw, kv_prev):
    n_tok = xp.shape[0]
    seq_len = 256
    n_seq_total = n_tok // seq_len
    seq_per_block = TOKEN_BLOCK // seq_len
    kv_shape = jax.ShapeDtypeStruct((n_seq_total, DEPTH, seq_len, C_W), F32)
    if kv_prev is None:
        kv_spec = pl.BlockSpec((seq_per_block, DEPTH, seq_len, C_W), lambda i: (i, 0, 0, 0))
    else:
        kv_spec = pl.BlockSpec((seq_per_block, 1, seq_len, C_W), lambda i: (i, layer, 0, 0))
    in_specs = [
        pl.BlockSpec((TOKEN_BLOCK, D_MODEL), lambda i: (i, 0)),
        _const_spec((1, 1, 3 * D_MODEL)),
        *_weight_specs(layer),
        _layer_spec((D_MODEL, D_MODEL), layer),
        _const_spec((1, D_MODEL)),
        _const_spec((1, D_MODEL)),
    ]
    args = [xp, mod, lw["w_in"], lw["conv_w"], lw["gln_g"], lw["gln_b"], lw["ws"], lw["bs"],
            lw["w_out"], lw["ln_g"], lw["ln_b"]]
    aliases = {}
    if kv_prev is not None:
        in_specs += [pl.BlockSpec(memory_space=pl.ANY), pl.BlockSpec(memory_space=pl.ANY)]
        aliases = {len(args): 1, len(args) + 1: 2}
        args += list(kv_prev)
    return pl.pallas_call(
        functools.partial(_ctx_layer_kernel, seq_len),
        grid=(n_tok // TOKEN_BLOCK,),
        in_specs=in_specs,
        out_specs=[pl.BlockSpec((TOKEN_BLOCK, D_MODEL), lambda i: (i, 0)), kv_spec, kv_spec],
        out_shape=[jax.ShapeDtypeStruct((n_tok, D_MODEL), F32), kv_shape, kv_shape],
        input_output_aliases=aliases,
        compiler_params=pltpu.CompilerParams(
            dimension_semantics=("arbitrary",), vmem_limit_bytes=VMEM_LIMIT_BYTES),
        name=f"ctx_layer_{layer}",
    )(*args)


def _lat_proj_kernel(seq_len, x_ref, xprev_ref, xnext_ref, mod_ref, w_in_ref, conv_w_ref, gln_g_ref,
                     gln_b_ref, ws_ref, bs_ref, yab_ref, qt_ref, k_ref, vt_ref, g_ref):
    tb = x_ref.shape[0]
    group_tokens = GROUP_ROWS * GRID_W
    h = _modulated_norm_parts(x_ref, mod_ref)
    h_halo = _modulated_norm(jnp.concatenate([xprev_ref[0], xnext_ref[0]], axis=0), mod_ref)
    pos = (pl.program_id(0) * tb + lax.broadcasted_iota(jnp.int32, (tb, 1), 0)) & (seq_len - 1)
    z_halo = _conv_halo(h_halo, w_in_ref)
    pb = _project(h, w_in_ref, B_COLS)
    pa = _project(h, w_in_ref, A_COLS)
    yb = _gmlp_branch(pb, gln_g_ref, gln_b_ref, ws_ref, bs_ref)
    pvg = _project(h, w_in_ref, (C_COLS[0] + 2 * C_W, C_COLS[1]))
    ya = _conv_branch(pa, z_halo, pos, seq_len, conv_w_ref)
    yab_ref[...] = jnp.concatenate([ya, yb], axis=1).astype(BF16)
    pq = _project(h, w_in_ref, (C_COLS[0], C_COLS[0] + C_W))
    v = pvg[:, 0:C_W]
    for t in range(N_HEAD_PAIRS):
        for n in range(tb // LANES):
            vt_ref[t, n] = v[n * LANES:(n + 1) * LANES, t * LANES:(t + 1) * LANES].T.astype(BF16)
    g_ref[...] = _silu(pvg[:, C_W:2 * C_W]).astype(BF16)
    k = _project(h, w_in_ref, (C_COLS[0] + C_W, C_COLS[0] + 2 * C_W))
    qs = pq * (ATTN_SCALE * LOG2_E)
    for t in range(N_HEAD_PAIRS):
        for n in range(tb // group_tokens):
            qt_ref[t, n] = qs[n * group_tokens:(n + 1) * group_tokens,
                              t * LANES:(t + 1) * LANES].T.astype(BF16)
    for t in range(N_HEAD_PAIRS):
        k_ref[t] = k[:, t * LANES:(t + 1) * LANES].astype(BF16)


def _lat_proj(layer, xs, mod, lw, seq_len):
    n_tok = xs.shape[0]
    blocks_per_seq = seq_len // TOKEN_BLOCK
    n_halo = n_tok // 8
    halo_per_block = TOKEN_BLOCK // 8
    xs8 = xs.reshape(n_halo, 8, D_MODEL)
    group_tokens = GROUP_ROWS * GRID_W
    pair_shape = jax.ShapeDtypeStruct((N_HEAD_PAIRS, n_tok, LANES), BF16)
    pair_spec = pl.BlockSpec((N_HEAD_PAIRS, TOKEN_BLOCK, LANES), lambda i: (0, i, 0))
    return pl.pallas_call(
        functools.partial(_lat_proj_kernel, seq_len),
        grid=(n_tok // TOKEN_BLOCK,),
        in_specs=[
            pl.BlockSpec((TOKEN_BLOCK, D_MODEL), lambda i: (i, 0)),
            pl.BlockSpec((1, 8, D_MODEL), lambda i: (jnp.maximum(i * halo_per_block - 1, 0), 0, 0)),
            pl.BlockSpec((1, 8, D_MODEL),
                         lambda i: (jnp.minimum((i + 1) * halo_per_block, n_halo - 1), 0, 0)),
            pl.BlockSpec((1, 1, 3 * D_MODEL), lambda i: (i // blocks_per_seq, 0, 0)),
            *_weight_specs(layer),
        ],
        out_specs=[
            pl.BlockSpec((TOKEN_BLOCK, A_W + B_W), lambda i: (i, 0)),
            pl.BlockSpec((N_HEAD_PAIRS, TOKEN_BLOCK // group_tokens, LANES, group_tokens),
                         lambda i: (0, i, 0, 0)),
            pair_spec,
            pl.BlockSpec((N_HEAD_PAIRS, TOKEN_BLOCK // LANES, LANES, LANES), lambda i: (0, i, 0, 0)),
            pl.BlockSpec((TOKEN_BLOCK, C_W), lambda i: (i, 0)),
        ],
        out_shape=[
            jax.ShapeDtypeStruct((n_tok, A_W + B_W), BF16),
            jax.ShapeDtypeStruct((N_HEAD_PAIRS, n_tok // group_tokens, LANES, group_tokens), BF16),
            pair_shape,
            jax.ShapeDtypeStruct((N_HEAD_PAIRS, n_tok // LANES, LANES, LANES), BF16),
            jax.ShapeDtypeStruct((n_tok, C_W), BF16),
        ],
        compiler_params=pltpu.CompilerParams(
            dimension_semantics=("arbitrary",), vmem_limit_bytes=VMEM_LIMIT_BYTES),
        name=f"lat_proj_{layer}",
    )(xs, xs8, xs8, mod, lw["w_in"], lw["conv_w"], lw["gln_g"], lw["gln_b"], lw["ws"], lw["bs"])


GROUP_INTERIOR, GROUP_FIRST, GROUP_LAST = 0, 1, 2


def _bias_entry_table():
    table = np.zeros((3, WINDOW_ROWS * (GROUP_ROWS // 2)), np.int32)
    for kind, base in ((GROUP_INTERIOR, 3), (GROUP_FIRST, 7), (GROUP_LAST, -1)):
        for j in range(WINDOW_ROWS):
            for ip in range(GROUP_ROWS // 2):
                e = base + j - 2 * ip
                if kind == GROUP_INTERIOR:
                    entry = (ENTRY_LOW_ONLY if e == 3 else ENTRY_HIGH_ONLY if e == 11
                             else ENTRY_MASKED if (e < 3 or e > 11) else e)
                elif kind == GROUP_FIRST:
                    entry = ENTRY_MASKED if j >= WIN_ROWS else e
                else:
                    entry = ENTRY_MASKED if j < WINDOW_ROWS - WIN_ROWS else e
                assert 0 <= entry < N_BIAS_ENTRIES
                table[kind, j * (GROUP_ROWS // 2) + ip] = entry
    return table


def _lat_attn_kernel(ent_ref, x_ref, yab_ref, qt_ref, g_ref, k_ref, vt_ref, ck_ref, cvt_ref, tbl_ref,
                     mod_ref, w_out_ref, ln_g_ref, ln_b_ref, xo_ref, s_scr, m_scr, ot_scr):
    rb = pl.program_id(1)
    tb = x_ref.shape[0]
    group_tokens = GROUP_ROWS * GRID_W
    window_tokens = WINDOW_ROWS * GRID_W
    window_tiles = window_tokens // LANES
    n_groups = (k_ref.shape[1] // GRID_W) // GROUP_ROWS
    groups_per_block = tb // group_tokens
    past = ck_ref.shape[2]
    n_units = N_HEAD_PAIRS * groups_per_block * 2
    group_bits = groups_per_block.bit_length() - 1
    assert groups_per_block == 1 << group_bits

    def coords(u):
        hp = lax.shift_right_logical(u, 1 + group_bits)
        gi = lax.shift_right_logical(u, 1) & (groups_per_block - 1)
        parity = u & 1
        g = rb * groups_per_block + gi
        first_tile = jnp.clip(2 * g - 2, 0, (GRID_W - WINDOW_ROWS) // 2)
        return hp, gi, parity, g, first_tile

    def score_stage(u, slot):
        hp, gi, parity, g, first_tile = coords(u)
        start = pl.multiple_of(first_tile * LANES, LANES)
        kw = k_ref[hp, pl.ds(start, window_tokens), :]
        qt = qt_ref[hp, gi]
        row_half = lax.shift_right_logical(
            lax.broadcasted_iota(jnp.int32, (LANES, group_tokens), 0), 6)
        qt_h = jnp.where(row_half == parity, qt, jnp.zeros_like(qt))
        half = (window_tokens + past) // 2
        s_a = jnp.dot(kw[0:half], qt_h, preferred_element_type=F32)
        s_b = jnp.dot(jnp.concatenate([kw[half:], ck_ref[hp, 0]], axis=0), qt_h,
                      preferred_element_type=F32)
        s_loc = jnp.concatenate([s_a, s_b[0:window_tokens - half]], axis=0)
        s_ctx = s_b[window_tokens - half:]
        head_base = (2 * hp + parity) * N_BIAS_ENTRIES
        kind = jnp.where(g == 0, GROUP_FIRST, jnp.where(g == n_groups - 1, GROUP_LAST, GROUP_INTERIOR))
        bias_rows = []
        for j in range(WINDOW_ROWS):
            tiles = []
            for ip in range(GROUP_ROWS // 2):
                tiles.append(tbl_ref[head_base + ent_ref[kind, j * (GROUP_ROWS // 2) + ip]])
            bias_rows.append(jnp.concatenate(tiles, axis=1))
        s_loc = s_loc + jnp.concatenate(bias_rows, axis=0)
        s_scr[slot, 0:window_tokens, :] = s_loc
        s_scr[slot, window_tokens:, :] = s_ctx
        m_scr[slot] = jnp.maximum(jnp.max(s_loc, axis=0, keepdims=True),
                                  jnp.max(s_ctx, axis=0, keepdims=True))

    def value_stage(u, slot):
        hp, gi, parity, g, first_tile = coords(u)
        p = jnp.exp2(s_scr[slot] - m_scr[slot]).astype(BF16)
        rows = pl.ds(pl.multiple_of(parity * C_HD, C_HD), C_HD)
        vt_tiles = vt_ref[hp, pl.ds(first_tile, window_tiles), rows, :]
        vtw = jnp.concatenate([vt_tiles[n] for n in range(window_tiles)], axis=1)
        ones_rows = 16
        lhs_loc = jnp.concatenate([vtw, jnp.ones((ones_rows, window_tokens), BF16)], axis=0)
        lhs_ctx = jnp.concatenate(
            [cvt_ref[hp, 0, rows, :], jnp.ones((ones_rows, s_scr.shape[1] - window_tokens), BF16)],
            axis=0)
        o_t = (jnp.dot(lhs_loc, p[0:window_tokens], preferred_element_type=F32)
               + jnp.dot(lhs_ctx, p[window_tokens:], preferred_element_type=F32))
        ot_scr[hp, gi, rows, :] = o_t[0:C_HD] / o_t[C_HD:C_HD + 1]

    def half_step(u, read_base):
        score_stage(u + 2, 2 - read_base)
        score_stage(u + 3, 3 - read_base)
        value_stage(u, read_base)
        value_stage(u + 1, read_base + 1)

    def pipeline_step(j, carry):
        half_step(4 * j, 0)
        half_step(4 * j + 2, 2)
        return carry

    score_stage(jnp.int32(0), 0)
    score_stage(jnp.int32(1), 1)
    n_steps = (n_units - 4) // 4
    lax.fori_loop(0, n_steps, pipeline_step, 0)
    half_step(jnp.int32(n_units - 4), 0)
    value_stage(jnp.int32(n_units - 2), 2)
    value_stage(jnp.int32(n_units - 1), 3)

    gate = mod_ref[0, :, 2 * D_MODEL:3 * D_MODEL]
    for gi in range(groups_per_block):
        rows = slice(gi * group_tokens, (gi + 1) * group_tokens)
        attn = jnp.concatenate([ot_scr[t, gi].T for t in range(N_HEAD_PAIRS)], axis=1)
        yc = (attn * g_ref[rows, :].astype(F32)).astype(BF16)
        y = jnp.concatenate([yab_ref[rows, :], yc], axis=1)
        xo_ref[rows, :] = _post_norm(x_ref[rows, :], y, gate, w_out_ref, ln_g_ref, ln_b_ref)


def _lat_attn(layer, xs, yab, q, k, vt, gc, ck, cvt, tbl, mod, lw, seq_len):
    n_tok = xs.shape[0]
    n_batch = n_tok // seq_len
    blocks_per_seq = seq_len // ATTN_TOKEN_BLOCK
    past = ck.shape[2]
    group_tokens = GROUP_ROWS * GRID_W
    tok_spec = lambda width: pl.BlockSpec((ATTN_TOKEN_BLOCK, width),
                                          lambda b, r: (b * blocks_per_seq + r, 0))
    return pl.pallas_call(
        _lat_attn_kernel,
        grid=(n_batch, blocks_per_seq),
        in_specs=[
            pl.BlockSpec(memory_space=pltpu.SMEM),
            tok_spec(D_MODEL),
            tok_spec(A_W + B_W),
            pl.BlockSpec((N_HEAD_PAIRS, ATTN_TOKEN_BLOCK // group_tokens, LANES, group_tokens),
                         lambda b, r: (0, b * blocks_per_seq + r, 0, 0)),
            tok_spec(C_W),
            pl.BlockSpec((N_HEAD_PAIRS, seq_len, LANES), lambda b, r: (0, b, 0)),
            pl.BlockSpec((N_HEAD_PAIRS, seq_len // LANES, LANES, LANES), lambda b, r: (0, b, 0, 0)),
            pl.BlockSpec((N_HEAD_PAIRS, 1, past, LANES), lambda b, r: (0, b, 0, 0)),
            pl.BlockSpec((N_HEAD_PAIRS, 1, LANES, past), lambda b, r: (0, b, 0, 0)),
            _layer_spec(tbl.shape[1:], layer),
            pl.BlockSpec((1, 1, 3 * D_MODEL), lambda b, r: (b, 0, 0)),
            _layer_spec((D_MODEL, D_MODEL), layer),
            _const_spec((1, D_MODEL)),
            _const_spec((1, D_MODEL)),
        ],
        out_specs=tok_spec(D_MODEL),
        out_shape=jax.ShapeDtypeStruct((n_tok, D_MODEL), F32),
        scratch_shapes=[
            pltpu.VMEM((4, WINDOW_ROWS * GRID_W + past, group_tokens), F32),
            pltpu.VMEM((4, 1, group_tokens), F32),
            pltpu.VMEM((N_HEAD_PAIRS, ATTN_TOKEN_BLOCK // group_tokens, LANES, group_tokens), F32),
        ],
        compiler_params=pltpu.CompilerParams(
            dimension_semantics=("arbitrary", "arbitrary"),
            vmem_limit_bytes=VMEM_LIMIT_BYTES),
        name=f"lat_attn_{layer}",
    )(jnp.asarray(_bias_entry_table()), xs, yab, q, gc, k, vt, ck, cvt, tbl, mod, lw["w_out"],
      lw["ln_g"], lw["ln_b"])


def _bias_table_kernel(rp_ref, o_ref):
    n_taps = 2 * WIN_COLS - 1
    n_dr = 2 * WIN_ROWS - 1
    cp = lax.broadcasted_iota(jnp.int32, (GRID_W, LANES), 0)
    lane = lax.broadcasted_iota(jnp.int32, (GRID_W, LANES), 1)
    c = lane & (GRID_W - 1)
    cs = jnp.clip(c - WIN_COLS // 2, 0, GRID_W - WIN_COLS)
    valid = (cp >= cs) & (cp < cs + WIN_COLS)
    low = lane < GRID_W
    neg = jnp.full((GRID_W, LANES), NEG_INF, F32)

    def diagonal(head, e):
        row = jnp.broadcast_to(rp_ref[0, head, e:e + 1, :], (GRID_W, LANES))
        return pltpu.roll(row, LANES - n_taps // 2, 1, stride=1, stride_axis=0) * LOG2_E

    for head in range(C_HEADS):
        diag = [diagonal(head, e) for e in range(n_dr)]

        def pair(e_low, e_high):
            lo = neg if e_low is None else diag[e_low]
            hi = neg if e_high is None else diag[e_high]
            return jnp.where(valid, jnp.where(low, lo, hi), NEG_INF)

        for e in range(n_dr + 1):
            o_ref[0, head, e] = pair(e if e < n_dr else None, e - 1 if e >= 1 else None)
        o_ref[0, head, ENTRY_MASKED] = neg
        o_ref[0, head, ENTRY_LOW_ONLY] = pair(3, None)
        o_ref[0, head, ENTRY_HIGH_ONLY] = pair(None, 10)


def _bias_tables(rpb):
    n_dr, n_taps = rpb.shape[2], rpb.shape[3]
    rp = jnp.pad(rpb[..., ::-1], ((0, 0), (0, 0), (0, 16 - n_dr), (0, GRID_W - n_taps)),
                 constant_values=NEG_INF)
    rp = jnp.concatenate([rp, rp], axis=-1)
    tbl = pl.pallas_call(
        _bias_table_kernel,
        grid=(DEPTH,),
        in_specs=[pl.BlockSpec((1, C_HEADS, 16, LANES), lambda l: (l, 0, 0, 0))],
        out_specs=pl.BlockSpec((1, C_HEADS, N_BIAS_ENTRIES, GRID_W, LANES), lambda l: (l, 0, 0, 0, 0)),
        out_shape=jax.ShapeDtypeStruct((DEPTH, C_HEADS, N_BIAS_ENTRIES, GRID_W, LANES), F32),
        compiler_params=pltpu.CompilerParams(dimension_semantics=("arbitrary",)),
        name="bias_tables",
    )(rp)
    return tbl.reshape(DEPTH, C_HEADS * N_BIAS_ENTRIES, GRID_W, LANES)


def _layer_weights(l, w_in_bf, w_out_bf, conv_w, gmlp_ln_g, gmlp_ln_b, ws_pairs, bs_lanes, ln_g, ln_b):
    return {
        "w_in": w_in_bf, "w_out": w_out_bf, "conv_w": conv_w[l],
        "gln_g": gmlp_ln_g[l][None], "gln_b": gmlp_ln_b[l][None],
        "ws": ws_pairs[l], "bs": bs_lanes[l],
        "ln_g": ln_g[l][None], "ln_b": ln_b[l][None],
    }


def kernel(x_prompt, x_sample, cache_k, cache_v, c, c_ctx, w_ada, b_ada, w_in, conv_w, gmlp_ln_g,
           gmlp_ln_b, w_spatial, b_spatial, rpb, w_out, ln_g, ln_b):
    batch, seq, d = x_prompt.shape
    dec_batch, dec_seq, _ = x_sample.shape
    past = cache_k.shape[2]

    n_mod_rows = 8
    cvec = jnp.concatenate(
        [c_ctx[None], c, jnp.zeros((n_mod_rows - 1 - dec_batch, d), F32)], axis=0)
    mod = _modulation(cvec, w_ada, b_ada)

    w_in_bf = w_in.astype(BF16)
    w_out_bf = w_out.astype(BF16)
    ws_pairs = w_spatial.reshape(DEPTH, 2, 2, CHUNK, CHUNK).transpose(0, 1, 3, 2, 4)
    ws_pairs = ws_pairs.reshape(DEPTH, 2, CHUNK, 2 * CHUNK).astype(BF16)
    bs_lanes = jnp.repeat(b_spatial.reshape(DEPTH, 2, 2, CHUNK).transpose(0, 1, 3, 2), C_HD, axis=-1)

    def pair_major(cache):
        cache = cache.reshape(dec_batch, DEPTH, past, N_HEAD_PAIRS, LANES)
        return cache.transpose(1, 3, 0, 2, 4).astype(BF16)

    ck_all = pair_major(cache_k)
    cvt_all = jnp.swapaxes(pair_major(cache_v), -1, -2)
    tables = _bias_tables(rpb)

    xp = x_prompt.reshape(batch * seq, d)
    xs = x_sample.reshape(dec_batch * dec_seq, d)
    kv = None
    for l in range(DEPTH):
        lw = _layer_weights(l, w_in_bf, w_out_bf, conv_w, gmlp_ln_g, gmlp_ln_b, ws_pairs, bs_lanes,
                            ln_g, ln_b)
        xp, new_k, new_v = _ctx_layer(l, xp, mod[l, 0:1][None], lw, kv)
        kv = (new_k, new_v)
        mod_lat = mod[l, 1:1 + dec_batch][:, None]
        yab, q, k, vt, gc = _lat_proj(l, xs, mod_lat, lw, dec_seq)
        xs = _lat_attn(l, xs, yab, q, k, vt, gc, ck_all[l], cvt_all[l], tables, mod_lat, lw, dec_seq)

    new_k, new_v = kv
    return (xp.reshape(batch, seq, d), xs.reshape(dec_batch, dec_seq, d),
            new_k.reshape(batch, DEPTH, seq, C_HEADS, C_HD),
            new_v.reshape(batch, DEPTH, seq, C_HEADS, C_HD))
```

```python
import functools

import jax
import jax.numpy as jnp
import numpy as np
from jax import lax
from jax.experimental import pallas as pl
from jax.experimental.pallas import tpu as pltpu

F32 = jnp.float32
BF16 = jnp.bfloat16

D_MODEL = 1024
DEPTH = 4
GRID_W = 64
A_W = 256
B_W = 256
CHUNK = 128
C_W = 512
C_HEADS = 8
C_HD = 64
WIN_ROWS = 8
WIN_COLS = 16
IN_W = 3840
ALPHA = (2 * DEPTH) ** 0.25
ATTN_SCALE = C_HD ** -0.5
NEG_INF = -1e30
LN_EPS = 1e-5
LOG2_E = 1.4426950408889634

LANES = 128
N_HEAD_PAIRS = C_W // LANES
A_COLS = (0, 4 * A_W)
B_COLS = (4 * A_W, 4 * A_W + 3 * B_W)
C_COLS = (4 * A_W + 3 * B_W, IN_W)

TOKEN_BLOCK = 512
ATTN_TOKEN_BLOCK = 1024
PROJ_TOKEN_BLOCK = 1024
SLAB_ROWS = 256
SCORE_LEAD_LAT = 2
SCORE_LEAD = 8
GROUP_ROWS = 4
WINDOW_ROWS = 12
ENTRY_MASKED = 16
ENTRY_LOW_ONLY = 17
ENTRY_HIGH_ONLY = 18
N_BIAS_ENTRIES = 19
VMEM_LIMIT_BYTES = 56 * 1024 * 1024


def _layernorm(x):
    mu = jnp.mean(x, axis=-1, keepdims=True)
    xc = x - mu
    var = jnp.mean(xc * xc, axis=-1, keepdims=True)
    return xc * lax.rsqrt(var + LN_EPS)


def _silu(x):
    return x / (1.0 + jnp.exp(-x))


def _lane_is_low(shape):
    return lax.broadcasted_iota(jnp.int32, shape, len(shape) - 1) < C_HD


def _mod_kernel(c_ref, w_ref, b_ref, o_ref):
    s = _silu(c_ref[...])
    w = w_ref[0]
    s_hi, w_hi = s.astype(BF16), w.astype(BF16)
    s_lo = (s - s_hi.astype(F32)).astype(BF16)
    w_lo = (w - w_hi.astype(F32)).astype(BF16)
    o_ref[0] = (jnp.dot(s_hi, w_hi, preferred_element_type=F32)
                + (jnp.dot(s_hi, w_lo, preferred_element_type=F32)
                   + jnp.dot(s_lo, w_hi, preferred_element_type=F32))) + b_ref[0]


def _modulation(cvec, w_ada, b_ada):
    n_rows = cvec.shape[0]
    col_block = 1024
    return pl.pallas_call(
        _mod_kernel,
        grid=(DEPTH, 3 * D_MODEL // col_block),
        in_specs=[
            pl.BlockSpec((n_rows, D_MODEL), lambda l, j: (0, 0)),
            pl.BlockSpec((1, D_MODEL, col_block), lambda l, j: (l, 0, j)),
            pl.BlockSpec((1, 1, col_block), lambda l, j: (l, 0, j)),
        ],
        out_specs=pl.BlockSpec((1, n_rows, col_block), lambda l, j: (l, 0, j)),
        out_shape=jax.ShapeDtypeStruct((DEPTH, n_rows, 3 * D_MODEL), F32),
        compiler_params=pltpu.CompilerParams(
            dimension_semantics=("arbitrary", "arbitrary"), vmem_limit_bytes=VMEM_LIMIT_BYTES),
        name="adaln_modulation",
    )(cvec, w_ada, b_ada.reshape(DEPTH, 1, 3 * D_MODEL))


def _modulated_norm(x, mod_ref):
    shift = mod_ref[0, :, 0:D_MODEL]
    scale = mod_ref[0, :, D_MODEL:2 * D_MODEL]
    return (_layernorm(x) * (1.0 + scale) + shift).astype(BF16)


def _modulated_norm_parts(x_ref, mod_ref):
    return [_modulated_norm(x_ref[lo:lo + SLAB_ROWS, :], mod_ref)
            for lo in range(0, x_ref.shape[0], SLAB_ROWS)]


def _project(h_parts, w_in_ref, cols):
    return jnp.concatenate(
        [jnp.dot(h, w_in_ref[:, cols[0]:cols[1]], preferred_element_type=F32) for h in h_parts],
        axis=0)


def _conv_halo(h_halo, w_in_ref):
    xh = jnp.dot(h_halo, w_in_ref[:, 0:A_W], preferred_element_type=F32)
    ch = jnp.dot(h_halo, w_in_ref[:, 2 * A_W:3 * A_W], preferred_element_type=F32)
    return ch * xh


def _conv_branch(pa, z_halo, pos, seq_len, conv_w_ref):
    tb = pa.shape[0]
    xa, ba = pa[:, 0:A_W], pa[:, A_W:2 * A_W]
    ca, ga = pa[:, 2 * A_W:3 * A_W], pa[:, 3 * A_W:4 * A_W]
    z = ca * xa
    z_prev = pltpu.roll(z, 1, 0)
    z_next = pltpu.roll(z, tb - 1, 0)
    if z_halo is not None:
        row = lax.broadcasted_iota(jnp.int32, (tb, 1), 0)
        z_prev = jnp.where(row == 0, z_halo[7:8, :], z_prev)
        z_next = jnp.where(row == tb - 1, z_halo[8:9, :], z_next)
    z_prev = jnp.where(pos == 0, 0.0, z_prev)
    z_next = jnp.where(pos == seq_len - 1, 0.0, z_next)
    cw = conv_w_ref[...]
    conv = cw[0:1, :] * z_prev + cw[1:2, :] * z + cw[2:3, :] * z_next
    return ba * conv * _silu(ga)


def _gmlp_branch(pb, gln_g_ref, gln_b_ref, ws_ref, bs_ref):
    tb = pb.shape[0]
    u, v, gb = pb[:, 0:B_W], pb[:, B_W:2 * B_W], pb[:, 2 * B_W:3 * B_W]
    vn = _layernorm(v) * gln_g_ref[...] + gln_b_ref[...]
    low = _lane_is_low((CHUNK, LANES))
    chunks = []
    for c in range(tb // CHUNK):
        tiles = []
        for t in range(B_W // LANES):
            vt = vn[c * CHUNK:(c + 1) * CHUNK, t * LANES:(t + 1) * LANES]
            rhs = jnp.concatenate([jnp.where(low, vt, 0.0), jnp.where(low, 0.0, vt)],
                                  axis=0).astype(BF16)
            tiles.append(jnp.dot(ws_ref[t], rhs, preferred_element_type=F32) + bs_ref[t])
        chunks.append(jnp.concatenate(tiles, axis=1))
    sv = jnp.concatenate(chunks, axis=0)
    return u * sv * _silu(gb)


def _post_norm(x, y_bf16, gate, w_out_ref, ln_g_ref, ln_b_ref):
    out = jnp.dot(y_bf16, w_out_ref[...], preferred_element_type=F32)
    return _layernorm(ALPHA * x + gate * out) * ln_g_ref[...] + ln_b_ref[...]


def _ctx_layer_kernel(seq_len, x_ref, mod_ref, w_in_ref, conv_w_ref, gln_g_ref, gln_b_ref, ws_ref,
                      bs_ref, w_out_ref, ln_g_ref, ln_b_ref, *rest):
    xo_ref, ko_ref, vo_ref = rest[-3:]
    tb = x_ref.shape[0]
    n_seq = tb // seq_len
    h = _modulated_norm_parts(x_ref, mod_ref)
    pos = lax.broadcasted_iota(jnp.int32, (tb, 1), 0) & (seq_len - 1)
    pb = _project(h, w_in_ref, B_COLS)
    pa = _project(h, w_in_ref, A_COLS)
    yb = _gmlp_branch(pb, gln_g_ref, gln_b_ref, ws_ref, bs_ref)
    pc = _project(h, w_in_ref, C_COLS)
    ya = _conv_branch(pa, None, pos, seq_len, conv_w_ref)

    q, k = pc[:, 0:C_W], pc[:, C_W:2 * C_W]
    v, gc = pc[:, 2 * C_W:3 * C_W], pc[:, 3 * C_W:4 * C_W]
    qs = q * (ATTN_SCALE * LOG2_E)
    gate = mod_ref[0, :, 2 * D_MODEL:3 * D_MODEL]
    row_half = lax.shift_right_logical(lax.broadcasted_iota(jnp.int32, (LANES, seq_len), 0), 6)
    ones_rows = jnp.ones((16, seq_len), BF16)

    for s in range(n_seq):
        rows = slice(s * seq_len, (s + 1) * seq_len)
        ko_ref[s, 0] = k[rows, :]
        vo_ref[s, 0] = v[rows, :]
        for later in range(1, ko_ref.shape[1]):
            ko_ref[s, later] = jnp.zeros((seq_len, C_W), F32)
            vo_ref[s, later] = jnp.zeros((seq_len, C_W), F32)
        def scores(head):
            lanes = slice((head // 2) * LANES, (head // 2 + 1) * LANES)
            k_t = k[rows, lanes].astype(BF16)
            qt = qs[rows, lanes].T.astype(BF16)
            qt_h = jnp.where(row_half == head % 2, qt, jnp.zeros_like(qt))
            return jnp.dot(k_t, qt_h, preferred_element_type=F32)

        def values(head, sc):
            lanes = slice((head // 2) * LANES, (head // 2 + 1) * LANES)
            vt = v[rows, lanes].T.astype(BF16)
            p = jnp.exp2(sc - jnp.max(sc, axis=0, keepdims=True)).astype(BF16)
            lhs = jnp.concatenate([vt[(head % 2) * C_HD:(head % 2 + 1) * C_HD, :], ones_rows], axis=0)
            o_t = jnp.dot(lhs, p, preferred_element_type=F32)
            return o_t[0:C_HD] / o_t[C_HD:C_HD + 1]

        pending, head_out = [], []
        for head in range(C_HEADS + SCORE_LEAD):
            if head < C_HEADS:
                pending.append(scores(head))
            if head >= SCORE_LEAD:
                head_out.append(values(head - SCORE_LEAD, pending[head - SCORE_LEAD]))
        attn = jnp.concatenate(
            [jnp.concatenate(head_out[2 * t:2 * t + 2], axis=0).T for t in range(N_HEAD_PAIRS)],
            axis=1)
        yc = attn * _silu(gc[rows, :])
        y = jnp.concatenate([ya[rows, :], yb[rows, :], yc], axis=1).astype(BF16)
        xo_ref[rows, :] = _post_norm(x_ref[rows, :], y, gate, w_out_ref, ln_g_ref, ln_b_ref)


def _const_spec(shape):
    return pl.BlockSpec(shape, lambda *_: (0,) * len(shape), pipeline_mode=pl.Buffered(1))


def _layer_spec(shape, layer):
    return pl.BlockSpec((None, *shape), lambda *_: (layer,) + (0,) * len(shape),
                        pipeline_mode=pl.Buffered(1))


def _weight_specs(layer):
    return [
        _layer_spec((D_MODEL, IN_W), layer),
        _const_spec((3, A_W)),
        _const_spec((1, B_W)),
        _const_spec((1, B_W)),
        _const_spec((B_W // LANES, CHUNK, 2 * CHUNK)),
        _const_spec((B_W // LANES, CHUNK, LANES)),
    ]


def _ctx_layer(layer, xp, mod, lw, kv_prev):
    n_tok = xp.shape[0]
    seq_len = 256
    n_seq_total = n_tok // seq_len
    kv_shape = jax.ShapeDtypeStruct((n_seq_total, DEPTH, seq_len, C_W), F32)
    block = TOKEN_BLOCK
    if kv_prev is None:
        kv_spec = pl.BlockSpec((block // seq_len, DEPTH, seq_len, C_W), lambda i: (i, 0, 0, 0))
    else:
        kv_spec = pl.BlockSpec((block // seq_len, 1, seq_len, C_W), lambda i: (i, layer, 0, 0))
    in_specs = [
        pl.BlockSpec((block, D_MODEL), lambda i: (i, 0)),
        _const_spec((1, 1, 3 * D_MODEL)),
        *_weight_specs(layer),
        _layer_spec((D_MODEL, D_MODEL), layer),
        _const_spec((1, D_MODEL)),
        _const_spec((1, D_MODEL)),
    ]
    args = [xp, mod, lw["w_in"], lw["conv_w"], lw["gln_g"], lw["gln_b"], lw["ws"], lw["bs"],
            lw["w_out"], lw["ln_g"], lw["ln_b"]]
    aliases = {}
    if kv_prev is not None:
        in_specs += [pl.BlockSpec(memory_space=pl.ANY), pl.BlockSpec(memory_space=pl.ANY)]
        aliases = {len(args): 1, len(args) + 1: 2}
        args += list(kv_prev)
    return pl.pallas_call(
        functools.partial(_ctx_layer_kernel, seq_len),
        grid=(n_tok // block,),
        in_specs=in_specs,
        out_specs=[pl.BlockSpec((block, D_MODEL), lambda i: (i, 0)), kv_spec, kv_spec],
        out_shape=[jax.ShapeDtypeStruct((n_tok, D_MODEL), F32), kv_shape, kv_shape],
        input_output_aliases=aliases,
        compiler_params=pltpu.CompilerParams(
            dimension_semantics=("arbitrary",), vmem_limit_bytes=VMEM_LIMIT_BYTES),
        name=f"ctx_layer_{layer}",
    )(*args)


def _lat_proj_kernel(seq_len, x_ref, xprev_ref, xnext_ref, mod_ref, w_in_ref, conv_w_ref, gln_g_ref,
                     gln_b_ref, ws_ref, bs_ref, yab_ref, qt_ref, k_ref, vt_ref, g_ref):
    tb = x_ref.shape[0]
    group_tokens = GROUP_ROWS * GRID_W
    h = _modulated_norm_parts(x_ref, mod_ref)
    h_halo = _modulated_norm(jnp.concatenate([xprev_ref[0], xnext_ref[0]], axis=0), mod_ref)
    pos = (pl.program_id(0) * tb + lax.broadcasted_iota(jnp.int32, (tb, 1), 0)) & (seq_len - 1)
    z_halo = _conv_halo(h_halo, w_in_ref)
    pb = _project(h, w_in_ref, B_COLS)
    pa = _project(h, w_in_ref, A_COLS)
    yb = _gmlp_branch(pb, gln_g_ref, gln_b_ref, ws_ref, bs_ref)
    pvg = _project(h, w_in_ref, (C_COLS[0] + 2 * C_W, C_COLS[1]))
    ya = _conv_branch(pa, z_halo, pos, seq_len, conv_w_ref)
    yab_ref[...] = jnp.concatenate([ya, yb], axis=1).astype(BF16)
    pq = _project(h, w_in_ref, (C_COLS[0], C_COLS[0] + C_W))
    v = pvg[:, 0:C_W]
    for t in range(N_HEAD_PAIRS):
        for n in range(tb // LANES):
            vt_ref[t, n] = v[n * LANES:(n + 1) * LANES, t * LANES:(t + 1) * LANES].T.astype(BF16)
    g_ref[...] = _silu(pvg[:, C_W:2 * C_W]).astype(BF16)
    k = _project(h, w_in_ref, (C_COLS[0] + C_W, C_COLS[0] + 2 * C_W))
    qs = pq * (ATTN_SCALE * LOG2_E)
    for t in range(N_HEAD_PAIRS):
        for n in range(tb // group_tokens):
            qt_ref[t, n] = qs[n * group_tokens:(n + 1) * group_tokens,
                              t * LANES:(t + 1) * LANES].T.astype(BF16)
    for t in range(N_HEAD_PAIRS):
        k_ref[t] = k[:, t * LANES:(t + 1) * LANES].astype(BF16)


def _lat_proj(layer, xs, mod, lw, seq_len):
    n_tok = xs.shape[0]
    block = PROJ_TOKEN_BLOCK
    blocks_per_seq = seq_len // block
    n_halo = n_tok // 8
    halo_per_block = block // 8
    xs8 = xs.reshape(n_halo, 8, D_MODEL)
    group_tokens = GROUP_ROWS * GRID_W
    pair_shape = jax.ShapeDtypeStruct((N_HEAD_PAIRS, n_tok, LANES), BF16)
    pair_spec = pl.BlockSpec((N_HEAD_PAIRS, block, LANES), lambda i: (0, i, 0))
    return pl.pallas_call(
        functools.partial(_lat_proj_kernel, seq_len),
        grid=(n_tok // block,),
        in_specs=[
            pl.BlockSpec((block, D_MODEL), lambda i: (i, 0)),
            pl.BlockSpec((1, 8, D_MODEL), lambda i: (jnp.maximum(i * halo_per_block - 1, 0), 0, 0)),
            pl.BlockSpec((1, 8, D_MODEL),
                         lambda i: (jnp.minimum((i + 1) * halo_per_block, n_halo - 1), 0, 0)),
            pl.BlockSpec((1, 1, 3 * D_MODEL), lambda i: (i // blocks_per_seq, 0, 0)),
            *_weight_specs(layer),
        ],
        out_specs=[
            pl.BlockSpec((block, A_W + B_W), lambda i: (i, 0)),
            pl.BlockSpec((N_HEAD_PAIRS, block // group_tokens, LANES, group_tokens),
                         lambda i: (0, i, 0, 0)),
            pair_spec,
            pl.BlockSpec((N_HEAD_PAIRS, block // LANES, LANES, LANES), lambda i: (0, i, 0, 0)),
            pl.BlockSpec((block, C_W), lambda i: (i, 0)),
        ],
        out_shape=[
            jax.ShapeDtypeStruct((n_tok, A_W + B_W), BF16),
            jax.ShapeDtypeStruct((N_HEAD_PAIRS, n_tok // group_tokens, LANES, group_tokens), BF16),
            pair_shape,
            jax.ShapeDtypeStruct((N_HEAD_PAIRS, n_tok // LANES, LANES, LANES), BF16),
            jax.ShapeDtypeStruct((n_tok, C_W), BF16),
        ],
        compiler_params=pltpu.CompilerParams(
            dimension_semantics=("arbitrary",), vmem_limit_bytes=VMEM_LIMIT_BYTES),
        name=f"lat_proj_{layer}",
    )(xs, xs8, xs8, mod, lw["w_in"], lw["conv_w"], lw["gln_g"], lw["gln_b"], lw["ws"], lw["bs"])


GROUP_INTERIOR, GROUP_FIRST, GROUP_LAST = 0, 1, 2


def _bias_entry_table():
    table = np.zeros((3, WINDOW_ROWS * (GROUP_ROWS // 2)), np.int32)
    for kind, base in ((GROUP_INTERIOR, 3), (GROUP_FIRST, 7), (GROUP_LAST, -1)):
        for j in range(WINDOW_ROWS):
            for ip in range(GROUP_ROWS // 2):
                e = base + j - 2 * ip
                if kind == GROUP_INTERIOR:
                    entry = (ENTRY_LOW_ONLY if e == 3 else ENTRY_HIGH_ONLY if e == 11
                             else ENTRY_MASKED if (e < 3 or e > 11) else e)
                elif kind == GROUP_FIRST:
                    entry = ENTRY_MASKED if j >= WIN_ROWS else e
                else:
                    entry = ENTRY_MASKED if j < WINDOW_ROWS - WIN_ROWS else e
                assert 0 <= entry < N_BIAS_ENTRIES
                table[kind, j * (GROUP_ROWS // 2) + ip] = entry
    return table


def _lat_attn_kernel(ent_ref, x_ref, yab_ref, qt_ref, g_ref, k_ref, vt_ref, ck_ref, cvt_ref, tbl_ref,
                     mod_ref, w_out_ref, ln_g_ref, ln_b_ref, xo_ref, s_scr, m_scr, ot_scr):
    rb = pl.program_id(1)
    tb = x_ref.shape[0]
    group_tokens = GROUP_ROWS * GRID_W
    window_tokens = WINDOW_ROWS * GRID_W
    window_tiles = window_tokens // LANES
    n_groups = (k_ref.shape[1] // GRID_W) // GROUP_ROWS
    groups_per_block = tb // group_tokens
    past = ck_ref.shape[2]
    n_units = N_HEAD_PAIRS * groups_per_block * 2
    group_bits = groups_per_block.bit_length() - 1
    assert groups_per_block == 1 << group_bits

    def coords(u):
        hp = lax.shift_right_logical(u, 1 + group_bits)
        gi = lax.shift_right_logical(u, 1) & (groups_per_block - 1)
        parity = u & 1
        g = rb * groups_per_block + gi
        first_tile = jnp.clip(2 * g - 2, 0, (GRID_W - WINDOW_ROWS) // 2)
        return hp, gi, parity, g, first_tile

    def score_stage(u, slot):
        hp, gi, parity, g, first_tile = coords(u)
        start = pl.multiple_of(first_tile * LANES, LANES)
        kw = k_ref[hp, pl.ds(start, window_tokens), :]
        qt = qt_ref[hp, gi]
        row_half = lax.shift_right_logical(
            lax.broadcasted_iota(jnp.int32, (LANES, group_tokens), 0), 6)
        qt_h = jnp.where(row_half == parity, qt, jnp.zeros_like(qt))
        half = (window_tokens + past) // 2
        s_a = jnp.dot(kw[0:half], qt_h, preferred_element_type=F32)
        s_b = jnp.dot(jnp.concatenate([kw[half:], ck_ref[hp, 0]], axis=0), qt_h,
                      preferred_element_type=F32)
        s_loc = jnp.concatenate([s_a, s_b[0:window_tokens - half]], axis=0)
        s_ctx = s_b[window_tokens - half:]
        head_base = (2 * hp + parity) * N_BIAS_ENTRIES
        kind = jnp.where(g == 0, GROUP_FIRST, jnp.where(g == n_groups - 1, GROUP_LAST, GROUP_INTERIOR))
        bias_rows = []
        for j in range(WINDOW_ROWS):
            tiles = []
            for ip in range(GROUP_ROWS // 2):
                tiles.append(tbl_ref[head_base + ent_ref[kind, j * (GROUP_ROWS // 2) + ip]])
            bias_rows.append(jnp.concatenate(tiles, axis=1))
        s_loc = s_loc + jnp.concatenate(bias_rows, axis=0)
        s_scr[slot, 0:window_tokens, :] = s_loc
        s_scr[slot, window_tokens:, :] = s_ctx
        m_scr[slot] = jnp.maximum(jnp.max(s_loc, axis=0, keepdims=True),
                                  jnp.max(s_ctx, axis=0, keepdims=True))

    def value_stage(u, slot):
        hp, gi, parity, g, first_tile = coords(u)
        p = jnp.exp2(s_scr[slot] - m_scr[slot]).astype(BF16)
        rows = pl.ds(pl.multiple_of(parity * C_HD, C_HD), C_HD)
        vt_tiles = vt_ref[hp, pl.ds(first_tile, window_tiles), rows, :]
        vtw = jnp.concatenate([vt_tiles[n] for n in range(window_tiles)], axis=1)
        ones_rows = 16
        lhs_loc = jnp.concatenate([vtw, jnp.ones((ones_rows, window_tokens), BF16)], axis=0)
        lhs_ctx = jnp.concatenate(
            [cvt_ref[hp, 0, rows, :], jnp.ones((ones_rows, s_scr.shape[1] - window_tokens), BF16)],
            axis=0)
        o_t = (jnp.dot(lhs_loc, p[0:window_tokens], preferred_element_type=F32)
               + jnp.dot(lhs_ctx, p[window_tokens:], preferred_element_type=F32))
        ot_scr[hp, gi, rows, :] = o_t[0:C_HD] / o_t[C_HD:C_HD + 1]

    n_slots = s_scr.shape[0]
    assert n_slots > SCORE_LEAD_LAT and n_units % n_slots == 0

    def pipeline_step(j, carry):
        for i in range(n_slots):
            score_stage(n_slots * j + i + SCORE_LEAD_LAT, (i + SCORE_LEAD_LAT) % n_slots)
            value_stage(n_slots * j + i, i)
        return carry

    for u in range(SCORE_LEAD_LAT):
        score_stage(jnp.int32(u), u)
    lax.fori_loop(0, n_units // n_slots - 1, pipeline_step, 0)
    for u in range(n_units - n_slots, n_units):
        if u + SCORE_LEAD_LAT < n_units:
            score_stage(jnp.int32(u + SCORE_LEAD_LAT), (u + SCORE_LEAD_LAT) % n_slots)
        value_stage(jnp.int32(u), u % n_slots)

    gate = mod_ref[0, :, 2 * D_MODEL:3 * D_MODEL]
    for gi in range(groups_per_block):
        rows = slice(gi * group_tokens, (gi + 1) * group_tokens)
        attn = jnp.concatenate([ot_scr[t, gi].T for t in range(N_HEAD_PAIRS)], axis=1)
        yc = (attn * g_ref[rows, :].astype(F32)).astype(BF16)
        y = jnp.concatenate([yab_ref[rows, :], yc], axis=1)
        xo_ref[rows, :] = _post_norm(x_ref[rows, :], y, gate, w_out_ref, ln_g_ref, ln_b_ref)


def _lat_attn(layer, xs, yab, q, k, vt, gc, ck, cvt, tbl, mod, lw, seq_len):
    n_tok = xs.shape[0]
    n_batch = n_tok // seq_len
    blocks_per_seq = seq_len // ATTN_TOKEN_BLOCK
    past = ck.shape[2]
    group_tokens = GROUP_ROWS * GRID_W
    tok_spec = lambda width: pl.BlockSpec((ATTN_TOKEN_BLOCK, width),
                                          lambda b, r: (b * blocks_per_seq + r, 0))
    return pl.pallas_call(
        _lat_attn_kernel,
        grid=(n_batch, blocks_per_seq),
        in_specs=[
            pl.BlockSpec(memory_space=pltpu.SMEM),
            tok_spec(D_MODEL),
            tok_spec(A_W + B_W),
            pl.BlockSpec((N_HEAD_PAIRS, ATTN_TOKEN_BLOCK // group_tokens, LANES, group_tokens),
                         lambda b, r: (0, b * blocks_per_seq + r, 0, 0)),
            tok_spec(C_W),
            pl.BlockSpec((N_HEAD_PAIRS, seq_len, LANES), lambda b, r: (0, b, 0)),
            pl.BlockSpec((N_HEAD_PAIRS, seq_len // LANES, LANES, LANES), lambda b, r: (0, b, 0, 0)),
            pl.BlockSpec((N_HEAD_PAIRS, 1, past, LANES), lambda b, r: (0, b, 0, 0)),
            pl.BlockSpec((N_HEAD_PAIRS, 1, LANES, past), lambda b, r: (0, b, 0, 0)),
            _layer_spec(tbl.shape[1:], layer),
            pl.BlockSpec((1, 1, 3 * D_MODEL), lambda b, r: (b, 0, 0)),
            _layer_spec((D_MODEL, D_MODEL), layer),
            _const_spec((1, D_MODEL)),
            _const_spec((1, D_MODEL)),
        ],
        out_specs=tok_spec(D_MODEL),
        out_shape=jax.ShapeDtypeStruct((n_tok, D_MODEL), F32),
        scratch_shapes=[
            pltpu.VMEM((4, WINDOW_ROWS * GRID_W + past, group_tokens), F32),
            pltpu.VMEM((4, 1, group_tokens), F32),
            pltpu.VMEM((N_HEAD_PAIRS, ATTN_TOKEN_BLOCK // group_tokens, LANES, group_tokens), F32),
        ],
        compiler_params=pltpu.CompilerParams(
            dimension_semantics=("arbitrary", "arbitrary"),
            vmem_limit_bytes=VMEM_LIMIT_BYTES),
        name=f"lat_attn_{layer}",
    )(jnp.asarray(_bias_entry_table()), xs, yab, q, gc, k, vt, ck, cvt, tbl, mod, lw["w_out"],
      lw["ln_g"], lw["ln_b"])


def _bias_table_kernel(rp_ref, o_ref):
    n_taps = 2 * WIN_COLS - 1
    n_dr = 2 * WIN_ROWS - 1
    cp = lax.broadcasted_iota(jnp.int32, (GRID_W, LANES), 0)
    lane = lax.broadcasted_iota(jnp.int32, (GRID_W, LANES), 1)
    c = lane & (GRID_W - 1)
    cs = jnp.clip(c - WIN_COLS // 2, 0, GRID_W - WIN_COLS)
    valid = (cp >= cs) & (cp < cs + WIN_COLS)
    low = lane < GRID_W
    neg = jnp.full((GRID_W, LANES), NEG_INF, F32)

    def diagonal(head, e):
        row = jnp.broadcast_to(rp_ref[0, head, e:e + 1, :], (GRID_W, LANES))
        return pltpu.roll(row, LANES - n_taps // 2, 1, stride=1, stride_axis=0) * LOG2_E

    for head in range(C_HEADS):
        diag = [diagonal(head, e) for e in range(n_dr)]

        def pair(e_low, e_high):
            lo = neg if e_low is None else diag[e_low]
            hi = neg if e_high is None else diag[e_high]
            return jnp.where(valid, jnp.where(low, lo, hi), NEG_INF)

        for e in range(n_dr + 1):
            o_ref[0, head, e] = pair(e if e < n_dr else None, e - 1 if e >= 1 else None)
        o_ref[0, head, ENTRY_MASKED] = neg
        o_ref[0, head, ENTRY_LOW_ONLY] = pair(3, None)
        o_ref[0, head, ENTRY_HIGH_ONLY] = pair(None, 10)


def _bias_tables(rpb):
    n_dr, n_taps = rpb.shape[2], rpb.shape[3]
    rp = jnp.pad(rpb[..., ::-1], ((0, 0), (0, 0), (0, 16 - n_dr), (0, GRID_W - n_taps)),
                 constant_values=NEG_INF)
    rp = jnp.concatenate([rp, rp], axis=-1)
    tbl = pl.pallas_call(
        _bias_table_kernel,
        grid=(DEPTH,),
        in_specs=[pl.BlockSpec((1, C_HEADS, 16, LANES), lambda l: (l, 0, 0, 0))],
        out_specs=pl.BlockSpec((1, C_HEADS, N_BIAS_ENTRIES, GRID_W, LANES), lambda l: (l, 0, 0, 0, 0)),
        out_shape=jax.ShapeDtypeStruct((DEPTH, C_HEADS, N_BIAS_ENTRIES, GRID_W, LANES), F32),
        compiler_params=pltpu.CompilerParams(dimension_semantics=("arbitrary",)),
        name="bias_tables",
    )(rp)
    return tbl.reshape(DEPTH, C_HEADS * N_BIAS_ENTRIES, GRID_W, LANES)


def _layer_weights(l, w_in_bf, w_out_bf, conv_w, gmlp_ln_g, gmlp_ln_b, ws_pairs, bs_lanes, ln_g, ln_b):
    return {
        "w_in": w_in_bf, "w_out": w_out_bf, "conv_w": conv_w[l],
        "gln_g": gmlp_ln_g[l][None], "gln_b": gmlp_ln_b[l][None],
        "ws": ws_pairs[l], "bs": bs_lanes[l],
        "ln_g": ln_g[l][None], "ln_b": ln_b[l][None],
    }


def kernel(x_prompt, x_sample, cache_k, cache_v, c, c_ctx, w_ada, b_ada, w_in, conv_w, gmlp_ln_g,
           gmlp_ln_b, w_spatial, b_spatial, rpb, w_out, ln_g, ln_b):
    batch, seq, d = x_prompt.shape
    dec_batch, dec_seq, _ = x_sample.shape
    past = cache_k.shape[2]

    n_mod_rows = 8
    cvec = jnp.concatenate(
        [c_ctx[None], c, jnp.zeros((n_mod_rows - 1 - dec_batch, d), F32)], axis=0)
    mod = _modulation(cvec, w_ada, b_ada)

    w_in_bf = w_in.astype(BF16)
    w_out_bf = w_out.astype(BF16)
    ws_pairs = w_spatial.reshape(DEPTH, 2, 2, CHUNK, CHUNK).transpose(0, 1, 3, 2, 4)
    ws_pairs = ws_pairs.reshape(DEPTH, 2, CHUNK, 2 * CHUNK).astype(BF16)
    bs_lanes = jnp.repeat(b_spatial.reshape(DEPTH, 2, 2, CHUNK).transpose(0, 1, 3, 2), C_HD, axis=-1)

    def pair_major(cache):
        cache = cache.reshape(dec_batch, DEPTH, past, N_HEAD_PAIRS, LANES)
        return cache.transpose(1, 3, 0, 2, 4).astype(BF16)

    ck_all = pair_major(cache_k)
    cvt_all = jnp.swapaxes(pair_major(cache_v), -1, -2)
    tables = _bias_tables(rpb)

    xp = x_prompt.reshape(batch * seq, d)
    xs = x_sample.reshape(dec_batch * dec_seq, d)
    kv = None
    for l in range(DEPTH):
        lw = _layer_weights(l, w_in_bf, w_out_bf, conv_w, gmlp_ln_g, gmlp_ln_b, ws_pairs, bs_lanes,
                            ln_g, ln_b)
        xp, new_k, new_v = _ctx_layer(l, xp, mod[l, 0:1][None], lw, kv)
        kv = (new_k, new_v)
        mod_lat = mod[l, 1:1 + dec_batch][:, None]
        yab, q, k, vt, gc = _lat_proj(l, xs, mod_lat, lw, dec_seq)
        xs = _lat_attn(l, xs, yab, q, k, vt, gc, ck_all[l], cvt_all[l], tables, mod_lat, lw, dec_seq)

    new_k, new_v = kv
    return (xp.reshape(batch, seq, d), xs.reshape(dec_batch, dec_seq, d),
            new_k.reshape(batch, DEPTH, seq, C_HEADS, C_HD),
            new_v.reshape(batch, DEPTH, seq, C_HEADS, C_HD))
```

```python
import functools

import jax
import jax.numpy as jnp
import numpy as np
from jax import lax
from jax.experimental import pallas as pl
from jax.experimental.pallas import tpu as pltpu

F32 = jnp.float32
BF16 = jnp.bfloat16

D_MODEL = 1024
DEPTH = 4
GRID_W = 64
A_W = 256
B_W = 256
CHUNK = 128
C_W = 512
C_HEADS = 8
C_HD = 64
WIN_ROWS = 8
WIN_COLS = 16
IN_W = 3840
ALPHA = (2 * DEPTH) ** 0.25
ATTN_SCALE = C_HD ** -0.5
NEG_INF = -1e30
LN_EPS = 1e-5
LOG2_E = 1.4426950408889634

LANES = 128
N_HEAD_PAIRS = C_W // LANES
A_COLS = (0, 4 * A_W)
B_COLS = (4 * A_W, 4 * A_W + 3 * B_W)
C_COLS = (4 * A_W + 3 * B_W, IN_W)

TOKEN_BLOCK = 512
ATTN_TOKEN_BLOCK = 1024
PROJ_TOKEN_BLOCK = 1024
SLAB_ROWS = 256
SCORE_LEAD_LAT = 2
SCORE_LEAD = 8
GROUP_ROWS = 4
WINDOW_ROWS = 12
ENTRY_MASKED = 16
ENTRY_LOW_ONLY = 17
ENTRY_HIGH_ONLY = 18
N_BIAS_ENTRIES = 19
VMEM_LIMIT_BYTES = 56 * 1024 * 1024


def _layernorm(x):
    mu = jnp.mean(x, axis=-1, keepdims=True)
    xc = x - mu
    var = jnp.mean(xc * xc, axis=-1, keepdims=True)
    return xc * lax.rsqrt(var + LN_EPS)


def _silu(x):
    return x / (1.0 + jnp.exp(-x))


def _lane_is_low(shape):
    return lax.broadcasted_iota(jnp.int32, shape, len(shape) - 1) < C_HD


def _mod_kernel(c_ref, w_ref, b_ref, o_ref):
    s = _silu(c_ref[...])
    w = w_ref[0]
    s_hi, w_hi = s.astype(BF16), w.astype(BF16)
    s_lo = (s - s_hi.astype(F32)).astype(BF16)
    w_lo = (w - w_hi.astype(F32)).astype(BF16)
    o_ref[0] = (jnp.dot(s_hi, w_hi, preferred_element_type=F32)
                + (jnp.dot(s_hi, w_lo, preferred_element_type=F32)
                   + jnp.dot(s_lo, w_hi, preferred_element_type=F32))) + b_ref[0]


def _modulation(cvec, w_ada, b_ada):
    n_rows = cvec.shape[0]
    col_block = 1024
    return pl.pallas_call(
        _mod_kernel,
        grid=(DEPTH, 3 * D_MODEL // col_block),
        in_specs=[
            pl.BlockSpec((n_rows, D_MODEL), lambda l, j: (0, 0)),
            pl.BlockSpec((1, D_MODEL, col_block), lambda l, j: (l, 0, j)),
            pl.BlockSpec((1, 1, col_block), lambda l, j: (l, 0, j)),
        ],
        out_specs=pl.BlockSpec((1, n_rows, col_block), lambda l, j: (l, 0, j)),
        out_shape=jax.ShapeDtypeStruct((DEPTH, n_rows, 3 * D_MODEL), F32),
        compiler_params=pltpu.CompilerParams(
            dimension_semantics=("arbitrary", "arbitrary"), vmem_limit_bytes=VMEM_LIMIT_BYTES),
        name="adaln_modulation",
    )(cvec, w_ada, b_ada.reshape(DEPTH, 1, 3 * D_MODEL))


def _modulated_norm(x, mod_ref):
    shift = mod_ref[0, :, 0:D_MODEL]
    scale = mod_ref[0, :, D_MODEL:2 * D_MODEL]
    return (_layernorm(x) * (1.0 + scale) + shift).astype(BF16)


def _modulated_norm_parts(x_ref, mod_ref):
    return [_modulated_norm(x_ref[lo:lo + SLAB_ROWS, :], mod_ref)
            for lo in range(0, x_ref.shape[0], SLAB_ROWS)]


def _project(h_parts, w_in_ref, cols):
    return jnp.concatenate(
        [jnp.dot(h, w_in_ref[:, cols[0]:cols[1]], preferred_element_type=F32) for h in h_parts],
        axis=0)


def _conv_halo(h_halo, w_in_ref):
    xh = jnp.dot(h_halo, w_in_ref[:, 0:A_W], preferred_element_type=F32)
    ch = jnp.dot(h_halo, w_in_ref[:, 2 * A_W:3 * A_W], preferred_element_type=F32)
    return ch * xh


def _conv_branch(pa, z_halo, pos, seq_len, conv_w_ref):
    tb = pa.shape[0]
    xa, ba = pa[:, 0:A_W], pa[:, A_W:2 * A_W]
    ca, ga = pa[:, 2 * A_W:3 * A_W], pa[:, 3 * A_W:4 * A_W]
    z = ca * xa
    z_prev = pltpu.roll(z, 1, 0)
    z_next = pltpu.roll(z, tb - 1, 0)
    if z_halo is not None:
        row = lax.broadcasted_iota(jnp.int32, (tb, 1), 0)
        z_prev = jnp.where(row == 0, z_halo[7:8, :], z_prev)
        z_next = jnp.where(row == tb - 1, z_halo[8:9, :], z_next)
    z_prev = jnp.where(pos == 0, 0.0, z_prev)
    z_next = jnp.where(pos == seq_len - 1, 0.0, z_next)
    cw = conv_w_ref[...]
    conv = cw[0:1, :] * z_prev + cw[1:2, :] * z + cw[2:3, :] * z_next
    return ba * conv * _silu(ga)


def _gmlp_branch(pb, gln_g_ref, gln_b_ref, ws_ref, bs_ref):
    tb = pb.shape[0]
    u, v, gb = pb[:, 0:B_W], pb[:, B_W:2 * B_W], pb[:, 2 * B_W:3 * B_W]
    vn = _layernorm(v) * gln_g_ref[...] + gln_b_ref[...]
    low = _lane_is_low((CHUNK, LANES))
    chunks = []
    for c in range(tb // CHUNK):
        tiles = []
        for t in range(B_W // LANES):
            vt = vn[c * CHUNK:(c + 1) * CHUNK, t * LANES:(t + 1) * LANES]
            rhs = jnp.concatenate([jnp.where(low, vt, 0.0), jnp.where(low, 0.0, vt)],
                                  axis=0).astype(BF16)
            tiles.append(jnp.dot(ws_ref[t], rhs, preferred_element_type=F32) + bs_ref[t])
        chunks.append(jnp.concatenate(tiles, axis=1))
    sv = jnp.concatenate(chunks, axis=0)
    return u * sv * _silu(gb)


def _post_norm(x, y_bf16, gate, w_out_ref, ln_g_ref, ln_b_ref):
    out = jnp.dot(y_bf16, w_out_ref[...], preferred_element_type=F32)
    return _layernorm(ALPHA * x + gate * out) * ln_g_ref[...] + ln_b_ref[...]


def _ctx_layer_kernel(seq_len, x_ref, mod_ref, w_in_ref, conv_w_ref, gln_g_ref, gln_b_ref, ws_ref,
                      bs_ref, w_out_ref, ln_g_ref, ln_b_ref, *rest):
    xo_ref, ko_ref, vo_ref = rest[-3:]
    tb = x_ref.shape[0]
    n_seq = tb // seq_len
    h = _modulated_norm_parts(x_ref, mod_ref)
    pos = lax.broadcasted_iota(jnp.int32, (tb, 1), 0) & (seq_len - 1)
    pb = _project(h, w_in_ref, B_COLS)
    pa = _project(h, w_in_ref, A_COLS)
    yb = _gmlp_branch(pb, gln_g_ref, gln_b_ref, ws_ref, bs_ref)
    pc = _project(h, w_in_ref, C_COLS)
    ya = _conv_branch(pa, None, pos, seq_len, conv_w_ref)

    q, k = pc[:, 0:C_W], pc[:, C_W:2 * C_W]
    v, gc = pc[:, 2 * C_W:3 * C_W], pc[:, 3 * C_W:4 * C_W]
    qs = q * (ATTN_SCALE * LOG2_E)
    gate = mod_ref[0, :, 2 * D_MODEL:3 * D_MODEL]
    row_half = lax.shift_right_logical(lax.broadcasted_iota(jnp.int32, (LANES, seq_len), 0), 6)
    ones_rows = jnp.ones((16, seq_len), BF16)

    for s in range(n_seq):
        rows = slice(s * seq_len, (s + 1) * seq_len)
        ko_ref[s, 0] = k[rows, :]
        vo_ref[s, 0] = v[rows, :]
        for later in range(1, ko_ref.shape[1]):
            ko_ref[s, later] = jnp.zeros((seq_len, C_W), F32)
            vo_ref[s, later] = jnp.zeros((seq_len, C_W), F32)

    def scores(s, head):
        rows = slice(s * seq_len, (s + 1) * seq_len)
        lanes = slice((head // 2) * LANES, (head // 2 + 1) * LANES)
        k_t = k[rows, lanes].astype(BF16)
        qt = qs[rows, lanes].T.astype(BF16)
        qt_h = jnp.where(row_half == head % 2, qt, jnp.zeros_like(qt))
        return jnp.dot(k_t, qt_h, preferred_element_type=F32)

    def values(s, head, sc):
        rows = slice(s * seq_len, (s + 1) * seq_len)
        lanes = slice((head // 2) * LANES, (head // 2 + 1) * LANES)
        vt = v[rows, lanes].T.astype(BF16)
        p = jnp.exp2(sc - jnp.max(sc, axis=0, keepdims=True)).astype(BF16)
        lhs = jnp.concatenate([vt[(head % 2) * C_HD:(head % 2 + 1) * C_HD, :], ones_rows], axis=0)
        o_t = jnp.dot(lhs, p, preferred_element_type=F32)
        return o_t[0:C_HD] / o_t[C_HD:C_HD + 1]

    def output(s, head_out):
        rows = slice(s * seq_len, (s + 1) * seq_len)
        attn = jnp.concatenate(
            [jnp.concatenate(head_out[2 * t:2 * t + 2], axis=0).T for t in range(N_HEAD_PAIRS)],
            axis=1)
        yc = attn * _silu(gc[rows, :])
        y = jnp.concatenate([ya[rows, :], yb[rows, :], yc], axis=1).astype(BF16)
        xo_ref[rows, :] = _post_norm(x_ref[rows, :], y, gate, w_out_ref, ln_g_ref, ln_b_ref)

    units = [(s, head) for s in range(n_seq) for head in range(C_HEADS)]
    pending, head_out = {}, {s: [] for s in range(n_seq)}
    for step in range(len(units) + SCORE_LEAD):
        if step < len(units):
            pending[units[step]] = scores(*units[step])
        if step >= SCORE_LEAD:
            s, head = units[step - SCORE_LEAD]
            head_out[s].append(values(s, head, pending.pop((s, head))))
            if head == C_HEADS - 1:
                output(s, head_out[s])


def _const_spec(shape):
    return pl.BlockSpec(shape, lambda *_: (0,) * len(shape), pipeline_mode=pl.Buffered(1))


def _layer_spec(shape, layer):
    return pl.BlockSpec((None, *shape), lambda *_: (layer,) + (0,) * len(shape),
                        pipeline_mode=pl.Buffered(1))


def _weight_specs(layer):
    return [
        _layer_spec((D_MODEL, IN_W), layer),
        _const_spec((3, A_W)),
        _const_spec((1, B_W)),
        _const_spec((1, B_W)),
        _const_spec((B_W // LANES, CHUNK, 2 * CHUNK)),
        _const_spec((B_W // LANES, CHUNK, LANES)),
    ]


def _ctx_layer(layer, xp, mod, lw, kv_prev):
    n_tok = xp.shape[0]
    seq_len = 256
    n_seq_total = n_tok // seq_len
    kv_shape = jax.ShapeDtypeStruct((n_seq_total, DEPTH, seq_len, C_W), F32)
    block = TOKEN_BLOCK
    if kv_prev is None:
        kv_spec = pl.BlockSpec((block // seq_len, DEPTH, seq_len, C_W), lambda i: (i, 0, 0, 0))
    else:
        kv_spec = pl.BlockSpec((block // seq_len, 1, seq_len, C_W), lambda i: (i, layer, 0, 0))
    in_specs = [
        pl.BlockSpec((block, D_MODEL), lambda i: (i, 0)),
        _const_spec((1, 1, 3 * D_MODEL)),
        *_weight_specs(layer),
        _layer_spec((D_MODEL, D_MODEL), layer),
        _const_spec((1, D_MODEL)),
        _const_spec((1, D_MODEL)),
    ]
    args = [xp, mod, lw["w_in"], lw["conv_w"], lw["gln_g"], lw["gln_b"], lw["ws"], lw["bs"],
            lw["w_out"], lw["ln_g"], lw["ln_b"]]
    aliases = {}
    if kv_prev is not None:
        in_specs += [pl.BlockSpec(memory_space=pl.ANY), pl.BlockSpec(memory_space=pl.ANY)]
        aliases = {len(args): 1, len(args) + 1: 2}
        args += list(kv_prev)
    return pl.pallas_call(
        functools.partial(_ctx_layer_kernel, seq_len),
        grid=(n_tok // block,),
        in_specs=in_specs,
        out_specs=[pl.BlockSpec((block, D_MODEL), lambda i: (i, 0)), kv_spec, kv_spec],
        out_shape=[jax.ShapeDtypeStruct((n_tok, D_MODEL), F32), kv_shape, kv_shape],
        input_output_aliases=aliases,
        compiler_params=pltpu.CompilerParams(
            dimension_semantics=("arbitrary",), vmem_limit_bytes=VMEM_LIMIT_BYTES),
        name=f"ctx_layer_{layer}",
    )(*args)


def _lat_proj_kernel(seq_len, x_ref, xprev_ref, xnext_ref, mod_ref, w_in_ref, conv_w_ref, gln_g_ref,
                     gln_b_ref, ws_ref, bs_ref, yab_ref, qt_ref, k_ref, vt_ref, g_ref):
    tb = x_ref.shape[0]
    group_tokens = GROUP_ROWS * GRID_W
    h = _modulated_norm_parts(x_ref, mod_ref)
    h_halo = _modulated_norm(jnp.concatenate([xprev_ref[0], xnext_ref[0]], axis=0), mod_ref)
    pos = (pl.program_id(0) * tb + lax.broadcasted_iota(jnp.int32, (tb, 1), 0)) & (seq_len - 1)
    z_halo = _conv_halo(h_halo, w_in_ref)
    pb = _project(h, w_in_ref, B_COLS)
    pa = _project(h, w_in_ref, A_COLS)
    yb = _gmlp_branch(pb, gln_g_ref, gln_b_ref, ws_ref, bs_ref)
    pvg = _project(h, w_in_ref, (C_COLS[0] + 2 * C_W, C_COLS[1]))
    ya = _conv_branch(pa, z_halo, pos, seq_len, conv_w_ref)
    yab_ref[...] = jnp.concatenate([ya, yb], axis=1).astype(BF16)
    pq = _project(h, w_in_ref, (C_COLS[0], C_COLS[0] + C_W))
    v = pvg[:, 0:C_W]
    for t in range(N_HEAD_PAIRS):
        for n in range(tb // LANES):
            vt_ref[t, n] = v[n * LANES:(n + 1) * LANES, t * LANES:(t + 1) * LANES].T.astype(BF16)
    g_ref[...] = _silu(pvg[:, C_W:2 * C_W]).astype(BF16)
    k = _project(h, w_in_ref, (C_COLS[0] + C_W, C_COLS[0] + 2 * C_W))
    qs = pq * (ATTN_SCALE * LOG2_E)
    for t in range(N_HEAD_PAIRS):
        for n in range(tb // group_tokens):
            qt_ref[t, n] = qs[n * group_tokens:(n + 1) * group_tokens,
                              t * LANES:(t + 1) * LANES].T.astype(BF16)
    for t in range(N_HEAD_PAIRS):
        k_ref[t] = k[:, t * LANES:(t + 1) * LANES].astype(BF16)


def _lat_proj(layer, xs, mod, lw, seq_len):
    n_tok = xs.shape[0]
    block = PROJ_TOKEN_BLOCK
    blocks_per_seq = seq_len // block
    n_halo = n_tok // 8
    halo_per_block = block // 8
    xs8 = xs.reshape(n_halo, 8, D_MODEL)
    group_tokens = GROUP_ROWS * GRID_W
    pair_shape = jax.ShapeDtypeStruct((N_HEAD_PAIRS, n_tok, LANES), BF16)
    pair_spec = pl.BlockSpec((N_HEAD_PAIRS, block, LANES), lambda i: (0, i, 0))
    return pl.pallas_call(
        functools.partial(_lat_proj_kernel, seq_len),
        grid=(n_tok // block,),
        in_specs=[
            pl.BlockSpec((block, D_MODEL), lambda i: (i, 0)),
            pl.BlockSpec((1, 8, D_MODEL), lambda i: (jnp.maximum(i * halo_per_block - 1, 0), 0, 0)),
            pl.BlockSpec((1, 8, D_MODEL),
                         lambda i: (jnp.minimum((i + 1) * halo_per_block, n_halo - 1), 0, 0)),
            pl.BlockSpec((1, 1, 3 * D_MODEL), lambda i: (i // blocks_per_seq, 0, 0)),
            *_weight_specs(layer),
        ],
        out_specs=[
            pl.BlockSpec((block, A_W + B_W), lambda i: (i, 0)),
            pl.BlockSpec((N_HEAD_PAIRS, block // group_tokens, LANES, group_tokens),
                         lambda i: (0, i, 0, 0)),
            pair_spec,
            pl.BlockSpec((N_HEAD_PAIRS, block // LANES, LANES, LANES), lambda i: (0, i, 0, 0)),
            pl.BlockSpec((block, C_W), lambda i: (i, 0)),
        ],
        out_shape=[
            jax.ShapeDtypeStruct((n_tok, A_W + B_W), BF16),
            jax.ShapeDtypeStruct((N_HEAD_PAIRS, n_tok // group_tokens, LANES, group_tokens), BF16),
            pair_shape,
            jax.ShapeDtypeStruct((N_HEAD_PAIRS, n_tok // LANES, LANES, LANES), BF16),
            jax.ShapeDtypeStruct((n_tok, C_W), BF16),
        ],
        compiler_params=pltpu.CompilerParams(
            dimension_semantics=("arbitrary",), vmem_limit_bytes=VMEM_LIMIT_BYTES),
        name=f"lat_proj_{layer}",
    )(xs, xs8, xs8, mod, lw["w_in"], lw["conv_w"], lw["gln_g"], lw["gln_b"], lw["ws"], lw["bs"])


GROUP_INTERIOR, GROUP_FIRST, GROUP_LAST = 0, 1, 2


def _bias_entry_table():
    table = np.zeros((3, WINDOW_ROWS * (GROUP_ROWS // 2)), np.int32)
    for kind, base in ((GROUP_INTERIOR, 3), (GROUP_FIRST, 7), (GROUP_LAST, -1)):
        for j in range(WINDOW_ROWS):
            for ip in range(GROUP_ROWS // 2):
                e = base + j - 2 * ip
                if kind == GROUP_INTERIOR:
                    entry = (ENTRY_LOW_ONLY if e == 3 else ENTRY_HIGH_ONLY if e == 11
                             else ENTRY_MASKED if (e < 3 or e > 11) else e)
                elif kind == GROUP_FIRST:
                    entry = ENTRY_MASKED if j >= WIN_ROWS else e
                else:
                    entry = ENTRY_MASKED if j < WINDOW_ROWS - WIN_ROWS else e
                assert 0 <= entry < N_BIAS_ENTRIES
                table[kind, j * (GROUP_ROWS // 2) + ip] = entry
    return table


def _lat_attn_kernel(ent_ref, x_ref, yab_ref, qt_ref, g_ref, k_ref, vt_ref, ck_ref, cvt_ref, tbl_ref,
                     mod_ref, w_out_ref, ln_g_ref, ln_b_ref, xo_ref, s_scr, m_scr, ot_scr):
    rb = pl.program_id(1)
    tb = x_ref.shape[0]
    group_tokens = GROUP_ROWS * GRID_W
    window_tokens = WINDOW_ROWS * GRID_W
    window_tiles = window_tokens // LANES
    n_groups = (k_ref.shape[1] // GRID_W) // GROUP_ROWS
    groups_per_block = tb // group_tokens
    n_units = N_HEAD_PAIRS * groups_per_block * 2
    group_bits = groups_per_block.bit_length() - 1
    assert groups_per_block == 1 << group_bits

    def coords(u):
        hp = lax.shift_right_logical(u, 1 + group_bits)
        gi = lax.shift_right_logical(u, 1) & (groups_per_block - 1)
        parity = u & 1
        g = rb * groups_per_block + gi
        first_tile = jnp.clip(2 * g - 2, 0, (GRID_W - WINDOW_ROWS) // 2)
        return hp, gi, parity, g, first_tile

    def score_stage(u, slot):
        hp, gi, parity, g, first_tile = coords(u)
        start = pl.multiple_of(first_tile * LANES, LANES)
        kw = k_ref[hp, pl.ds(start, window_tokens), :]
        qt = qt_ref[hp, gi]
        row_half = lax.shift_right_logical(
            lax.broadcasted_iota(jnp.int32, (LANES, group_tokens), 0), 6)
        qt_h = jnp.where(row_half == parity, qt, jnp.zeros_like(qt))
        key_chunk = 256
        s_loc = jnp.concatenate(
            [jnp.dot(kw[c:c + key_chunk], qt_h, preferred_element_type=F32)
             for c in range(0, window_tokens, key_chunk)], axis=0)
        s_ctx = jnp.dot(ck_ref[hp, 0], qt_h, preferred_element_type=F32)
        head_base = (2 * hp + parity) * N_BIAS_ENTRIES
        kind = jnp.where(g == 0, GROUP_FIRST, jnp.where(g == n_groups - 1, GROUP_LAST, GROUP_INTERIOR))
        bias_rows = []
        for j in range(WINDOW_ROWS):
            tiles = []
            for ip in range(GROUP_ROWS // 2):
                tiles.append(tbl_ref[head_base + ent_ref[kind, j * (GROUP_ROWS // 2) + ip]])
            bias_rows.append(jnp.concatenate(tiles, axis=1))
        s_loc = s_loc + jnp.concatenate(bias_rows, axis=0)
        s_scr[slot, 0:window_tokens, :] = s_loc
        s_scr[slot, window_tokens:, :] = s_ctx
        m_scr[slot] = jnp.maximum(jnp.max(s_loc, axis=0, keepdims=True),
                                  jnp.max(s_ctx, axis=0, keepdims=True))

    def value_stage(u, slot):
        hp, gi, parity, g, first_tile = coords(u)
        p = jnp.exp2(s_scr[slot] - m_scr[slot]).astype(BF16)
        rows = pl.ds(pl.multiple_of(parity * C_HD, C_HD), C_HD)
        vt_tiles = vt_ref[hp, pl.ds(first_tile, window_tiles), rows, :]
        vtw = jnp.concatenate([vt_tiles[n] for n in range(window_tiles)] + [cvt_ref[hp, 0, rows, :]],
                              axis=1)
        lhs = jnp.concatenate([vtw, jnp.ones((16, s_scr.shape[1]), BF16)], axis=0)
        o_t = jnp.dot(lhs, p, preferred_element_type=F32)
        ot_scr[hp, gi, rows, :] = o_t[0:C_HD] / o_t[C_HD:C_HD + 1]

    n_slots = s_scr.shape[0]
    assert n_slots > SCORE_LEAD_LAT and n_units % n_slots == 0

    def pipeline_step(j, carry):
        for i in range(n_slots):
            score_stage(n_slots * j + i + SCORE_LEAD_LAT, (i + SCORE_LEAD_LAT) % n_slots)
            value_stage(n_slots * j + i, i)
        return carry

    gate = mod_ref[0, :, 2 * D_MODEL:3 * D_MODEL]

    def output_stage(gi):
        rows = slice(gi * group_tokens, (gi + 1) * group_tokens)
        attn = jnp.concatenate([ot_scr[t, gi].T for t in range(N_HEAD_PAIRS)], axis=1)
        yc = (attn * g_ref[rows, :].astype(F32)).astype(BF16)
        y = jnp.concatenate([yab_ref[rows, :], yc], axis=1)
        xo_ref[rows, :] = _post_norm(x_ref[rows, :], y, gate, w_out_ref, ln_g_ref, ln_b_ref)

    n_tail = 2 * groups_per_block
    assert (n_units - n_tail) % n_slots == 0
    for u in range(SCORE_LEAD_LAT):
        score_stage(jnp.int32(u), u)
    lax.fori_loop(0, (n_units - n_tail) // n_slots, pipeline_step, 0)
    for u in range(n_units - n_tail, n_units):
        if u + SCORE_LEAD_LAT < n_units:
            score_stage(jnp.int32(u + SCORE_LEAD_LAT), (u + SCORE_LEAD_LAT) % n_slots)
        value_stage(jnp.int32(u), u % n_slots)
        if u % 2 == 1:
            output_stage((u - (n_units - n_tail)) // 2)


def _lat_attn(layer, xs, yab, q, k, vt, gc, ck, cvt, tbl, mod, lw, seq_len):
    n_tok = xs.shape[0]
    n_batch = n_tok // seq_len
    blocks_per_seq = seq_len // ATTN_TOKEN_BLOCK
    past = ck.shape[2]
    group_tokens = GROUP_ROWS * GRID_W
    tok_spec = lambda width: pl.BlockSpec((ATTN_TOKEN_BLOCK, width),
                                          lambda b, r: (b * blocks_per_seq + r, 0))
    return pl.pallas_call(
        _lat_attn_kernel,
        grid=(n_batch, blocks_per_seq),
        in_specs=[
            pl.BlockSpec(memory_space=pltpu.SMEM),
            tok_spec(D_MODEL),
            tok_spec(A_W + B_W),
            pl.BlockSpec((N_HEAD_PAIRS, ATTN_TOKEN_BLOCK // group_tokens, LANES, group_tokens),
                         lambda b, r: (0, b * blocks_per_seq + r, 0, 0)),
            tok_spec(C_W),
            pl.BlockSpec((N_HEAD_PAIRS, seq_len, LANES), lambda b, r: (0, b, 0)),
            pl.BlockSpec((N_HEAD_PAIRS, seq_len // LANES, LANES, LANES), lambda b, r: (0, b, 0, 0)),
            pl.BlockSpec((N_HEAD_PAIRS, 1, past, LANES), lambda b, r: (0, b, 0, 0)),
            pl.BlockSpec((N_HEAD_PAIRS, 1, LANES, past), lambda b, r: (0, b, 0, 0)),
            _layer_spec(tbl.shape[1:], layer),
            pl.BlockSpec((1, 1, 3 * D_MODEL), lambda b, r: (b, 0, 0)),
            _layer_spec((D_MODEL, D_MODEL), layer),
            _const_spec((1, D_MODEL)),
            _const_spec((1, D_MODEL)),
        ],
        out_specs=tok_spec(D_MODEL),
        out_shape=jax.ShapeDtypeStruct((n_tok, D_MODEL), F32),
        scratch_shapes=[
            pltpu.VMEM((4, WINDOW_ROWS * GRID_W + past, group_tokens), F32),
            pltpu.VMEM((4, 1, group_tokens), F32),
            pltpu.VMEM((N_HEAD_PAIRS, ATTN_TOKEN_BLOCK // group_tokens, LANES, group_tokens), F32),
        ],
        compiler_params=pltpu.CompilerParams(
            dimension_semantics=("arbitrary", "arbitrary"),
            vmem_limit_bytes=VMEM_LIMIT_BYTES),
        name=f"lat_attn_{layer}",
    )(jnp.asarray(_bias_entry_table()), xs, yab, q, gc, k, vt, ck, cvt, tbl, mod, lw["w_out"],
      lw["ln_g"], lw["ln_b"])


def _bias_table_kernel(rp_ref, o_ref):
    n_taps = 2 * WIN_COLS - 1
    n_dr = 2 * WIN_ROWS - 1
    cp = lax.broadcasted_iota(jnp.int32, (GRID_W, LANES), 0)
    lane = lax.broadcasted_iota(jnp.int32, (GRID_W, LANES), 1)
    c = lane & (GRID_W - 1)
    cs = jnp.clip(c - WIN_COLS // 2, 0, GRID_W - WIN_COLS)
    valid = (cp >= cs) & (cp < cs + WIN_COLS)
    low = lane < GRID_W
    neg = jnp.full((GRID_W, LANES), NEG_INF, F32)

    def diagonal(head, e):
        row = jnp.broadcast_to(rp_ref[0, head, e:e + 1, :], (GRID_W, LANES))
        return pltpu.roll(row, LANES - n_taps // 2, 1, stride=1, stride_axis=0) * LOG2_E

    for head in range(C_HEADS):
        diag = [diagonal(head, e) for e in range(n_dr)]

        def pair(e_low, e_high):
            lo = neg if e_low is None else diag[e_low]
            hi = neg if e_high is None else diag[e_high]
            return jnp.where(valid, jnp.where(low, lo, hi), NEG_INF)

        for e in range(n_dr + 1):
            o_ref[0, head, e] = pair(e if e < n_dr else None, e - 1 if e >= 1 else None)
        o_ref[0, head, ENTRY_MASKED] = neg
        o_ref[0, head, ENTRY_LOW_ONLY] = pair(3, None)
        o_ref[0, head, ENTRY_HIGH_ONLY] = pair(None, 10)


def _bias_tables(rpb):
    n_dr, n_taps = rpb.shape[2], rpb.shape[3]
    rp = jnp.pad(rpb[..., ::-1], ((0, 0), (0, 0), (0, 16 - n_dr), (0, GRID_W - n_taps)),
                 constant_values=NEG_INF)
    rp = jnp.concatenate([rp, rp], axis=-1)
    tbl = pl.pallas_call(
        _bias_table_kernel,
        grid=(DEPTH,),
        in_specs=[pl.BlockSpec((1, C_HEADS, 16, LANES), lambda l: (l, 0, 0, 0))],
        out_specs=pl.BlockSpec((1, C_HEADS, N_BIAS_ENTRIES, GRID_W, LANES), lambda l: (l, 0, 0, 0, 0)),
        out_shape=jax.ShapeDtypeStruct((DEPTH, C_HEADS, N_BIAS_ENTRIES, GRID_W, LANES), F32),
        compiler_params=pltpu.CompilerParams(dimension_semantics=("arbitrary",)),
        name="bias_tables",
    )(rp)
    return tbl.reshape(DEPTH, C_HEADS * N_BIAS_ENTRIES, GRID_W, LANES)


def _layer_weights(l, w_in_bf, w_out_bf, conv_w, gmlp_ln_g, gmlp_ln_b, ws_pairs, bs_lanes, ln_g, ln_b):
    return {
        "w_in": w_in_bf, "w_out": w_out_bf, "conv_w": conv_w[l],
        "gln_g": gmlp_ln_g[l][None], "gln_b": gmlp_ln_b[l][None],
        "ws": ws_pairs[l], "bs": bs_lanes[l],
        "ln_g": ln_g[l][None], "ln_b": ln_b[l][None],
    }


def kernel(x_prompt, x_sample, cache_k, cache_v, c, c_ctx, w_ada, b_ada, w_in, conv_w, gmlp_ln_g,
           gmlp_ln_b, w_spatial, b_spatial, rpb, w_out, ln_g, ln_b):
    batch, seq, d = x_prompt.shape
    dec_batch, dec_seq, _ = x_sample.shape
    past = cache_k.shape[2]

    n_mod_rows = 8
    cvec = jnp.concatenate(
        [c_ctx[None], c, jnp.zeros((n_mod_rows - 1 - dec_batch, d), F32)], axis=0)
    mod = _modulation(cvec, w_ada, b_ada)

    w_in_bf = w_in.astype(BF16)
    w_out_bf = w_out.astype(BF16)
    ws_pairs = w_spatial.reshape(DEPTH, 2, 2, CHUNK, CHUNK).transpose(0, 1, 3, 2, 4)
    ws_pairs = ws_pairs.reshape(DEPTH, 2, CHUNK, 2 * CHUNK).astype(BF16)
    bs_lanes = jnp.repeat(b_spatial.reshape(DEPTH, 2, 2, CHUNK).transpose(0, 1, 3, 2), C_HD, axis=-1)

    def pair_major(cache):
        cache = cache.reshape(dec_batch, DEPTH, past, N_HEAD_PAIRS, LANES)
        return cache.transpose(1, 3, 0, 2, 4).astype(BF16)

    ck_all = pair_major(cache_k)
    cvt_all = jnp.swapaxes(pair_major(cache_v), -1, -2)
    tables = _bias_tables(rpb)

    xp = x_prompt.reshape(batch * seq, d)
    xs = x_sample.reshape(dec_batch * dec_seq, d)
    kv = None
    for l in range(DEPTH):
        lw = _layer_weights(l, w_in_bf, w_out_bf, conv_w, gmlp_ln_g, gmlp_ln_b, ws_pairs, bs_lanes,
                            ln_g, ln_b)
        xp, new_k, new_v = _ctx_layer(l, xp, mod[l, 0:1][None], lw, kv)
        kv = (new_k, new_v)
        mod_lat = mod[l, 1:1 + dec_batch][:, None]
        yab, q, k, vt, gc = _lat_proj(l, xs, mod_lat, lw, dec_seq)
        xs = _lat_attn(l, xs, yab, q, k, vt, gc, ck_all[l], cvt_all[l], tables, mod_lat, lw, dec_seq)

    new_k, new_v = kv
    return (xp.reshape(batch, seq, d), xs.reshape(dec_batch, dec_seq, d),
            new_k.reshape(batch, DEPTH, seq, C_HEADS, C_HD),
            new_v.reshape(batch, DEPTH, seq, C_HEADS, C_HD))
```

```python
import functools

import jax
import jax.numpy as jnp
import numpy as np
from jax import lax
from jax.experimental import pallas as pl
from jax.experimental.pallas import tpu as pltpu

F32 = jnp.float32
BF16 = jnp.bfloat16

D_MODEL = 1024
DEPTH = 4
GRID_W = 64
A_W = 256
B_W = 256
CHUNK = 128
C_W = 512
C_HEADS = 8
C_HD = 64
WIN_ROWS = 8
WIN_COLS = 16
IN_W = 3840
ALPHA = (2 * DEPTH) ** 0.25
ATTN_SCALE = C_HD ** -0.5
NEG_INF = -1e30
LN_EPS = 1e-5
LOG2_E = 1.4426950408889634

LANES = 128
N_HEAD_PAIRS = C_W // LANES
A_COLS = (0, 4 * A_W)
B_COLS = (4 * A_W, 4 * A_W + 3 * B_W)
C_COLS = (4 * A_W + 3 * B_W, IN_W)

TOKEN_BLOCK = 512
ATTN_TOKEN_BLOCK = 1024
PROJ_TOKEN_BLOCK = 2048
CTX_TOKEN_BLOCK = 1024
SLAB_ROWS = 256
SCORE_LEAD_LAT = 2
SCORE_LEAD = 8
GROUP_ROWS = 4
WINDOW_ROWS = 12
ENTRY_MASKED = 16
ENTRY_LOW_ONLY = 17
ENTRY_HIGH_ONLY = 18
N_BIAS_ENTRIES = 19
VMEM_LIMIT_BYTES = 58 * 1024 * 1024


def _layernorm(x):
    mu = jnp.mean(x, axis=-1, keepdims=True)
    xc = x - mu
    var = jnp.mean(xc * xc, axis=-1, keepdims=True)
    return xc * lax.rsqrt(var + LN_EPS)


def _silu(x):
    return x / (1.0 + jnp.exp(-x))


def _lane_is_low(shape):
    return lax.broadcasted_iota(jnp.int32, shape, len(shape) - 1) < C_HD


def _mod_kernel(c_ref, w_ref, b_ref, o_ref):
    s = _silu(c_ref[...])
    w = w_ref[0]
    s_hi, w_hi = s.astype(BF16), w.astype(BF16)
    s_lo = (s - s_hi.astype(F32)).astype(BF16)
    w_lo = (w - w_hi.astype(F32)).astype(BF16)
    o_ref[0] = (jnp.dot(s_hi, w_hi, preferred_element_type=F32)
                + (jnp.dot(s_hi, w_lo, preferred_element_type=F32)
                   + jnp.dot(s_lo, w_hi, preferred_element_type=F32))) + b_ref[0]


def _modulation(cvec, w_ada, b_ada):
    n_rows = cvec.shape[0]
    col_block = 1024
    return pl.pallas_call(
        _mod_kernel,
        grid=(DEPTH, 3 * D_MODEL // col_block),
        in_specs=[
            pl.BlockSpec((n_rows, D_MODEL), lambda l, j: (0, 0)),
            pl.BlockSpec((1, D_MODEL, col_block), lambda l, j: (l, 0, j)),
            pl.BlockSpec((1, 1, col_block), lambda l, j: (l, 0, j)),
        ],
        out_specs=pl.BlockSpec((1, n_rows, col_block), lambda l, j: (l, 0, j)),
        out_shape=jax.ShapeDtypeStruct((DEPTH, n_rows, 3 * D_MODEL), F32),
        compiler_params=pltpu.CompilerParams(
            dimension_semantics=("arbitrary", "arbitrary"), vmem_limit_bytes=VMEM_LIMIT_BYTES),
        name="adaln_modulation",
    )(cvec, w_ada, b_ada.reshape(DEPTH, 1, 3 * D_MODEL))


def _modulated_norm(x, mod_ref):
    shift = mod_ref[0, :, 0:D_MODEL]
    scale = mod_ref[0, :, D_MODEL:2 * D_MODEL]
    return (_layernorm(x) * (1.0 + scale) + shift).astype(BF16)


def _modulated_norm_parts(x_ref, mod_ref):
    return [_modulated_norm(x_ref[lo:lo + SLAB_ROWS, :], mod_ref)
            for lo in range(0, x_ref.shape[0], SLAB_ROWS)]


def _project(h_parts, w_in_ref, cols):
    return jnp.concatenate(
        [jnp.dot(h, w_in_ref[:, cols[0]:cols[1]], preferred_element_type=F32) for h in h_parts],
        axis=0)


def _conv_halo(h_halo, w_in_ref):
    xh = jnp.dot(h_halo, w_in_ref[:, 0:A_W], preferred_element_type=F32)
    ch = jnp.dot(h_halo, w_in_ref[:, 2 * A_W:3 * A_W], preferred_element_type=F32)
    return ch * xh


def _conv_branch(pa, z_halo, pos, seq_len, conv_w_ref):
    tb = pa.shape[0]
    xa, ba = pa[:, 0:A_W], pa[:, A_W:2 * A_W]
    ca, ga = pa[:, 2 * A_W:3 * A_W], pa[:, 3 * A_W:4 * A_W]
    z = ca * xa
    z_prev = pltpu.roll(z, 1, 0)
    z_next = pltpu.roll(z, tb - 1, 0)
    if z_halo is not None:
        row = lax.broadcasted_iota(jnp.int32, (tb, 1), 0)
        z_prev = jnp.where(row == 0, z_halo[7:8, :], z_prev)
        z_next = jnp.where(row == tb - 1, z_halo[8:9, :], z_next)
    z_prev = jnp.where(pos == 0, 0.0, z_prev)
    z_next = jnp.where(pos == seq_len - 1, 0.0, z_next)
    cw = conv_w_ref[...]
    conv = cw[0:1, :] * z_prev + cw[1:2, :] * z + cw[2:3, :] * z_next
    return ba * conv * _silu(ga)


def _gmlp_branch(pb, gln_g_ref, gln_b_ref, ws_ref, bs_ref):
    tb = pb.shape[0]
    u, v, gb = pb[:, 0:B_W], pb[:, B_W:2 * B_W], pb[:, 2 * B_W:3 * B_W]
    vn = _layernorm(v) * gln_g_ref[...] + gln_b_ref[...]
    low = _lane_is_low((CHUNK, LANES))
    chunks = []
    for c in range(tb // CHUNK):
        tiles = []
        for t in range(B_W // LANES):
            vt = vn[c * CHUNK:(c + 1) * CHUNK, t * LANES:(t + 1) * LANES]
            rhs = jnp.concatenate([jnp.where(low, vt, 0.0), jnp.where(low, 0.0, vt)],
                                  axis=0).astype(BF16)
            tiles.append(jnp.dot(ws_ref[t], rhs, preferred_element_type=F32) + bs_ref[t])
        chunks.append(jnp.concatenate(tiles, axis=1))
    sv = jnp.concatenate(chunks, axis=0)
    return u * sv * _silu(gb)


def _post_norm(x, y_bf16, gate, w_out_ref, ln_g_ref, ln_b_ref):
    out = jnp.dot(y_bf16, w_out_ref[...], preferred_element_type=F32)
    return _layernorm(ALPHA * x + gate * out) * ln_g_ref[...] + ln_b_ref[...]


def _ctx_layer_kernel(seq_len, x_ref, mod_ref, w_in_ref, conv_w_ref, gln_g_ref, gln_b_ref, ws_ref,
                      bs_ref, w_out_ref, ln_g_ref, ln_b_ref, *rest):
    xo_ref, ko_ref, vo_ref = rest[-3:]
    tb = x_ref.shape[0]
    n_seq = tb // seq_len
    h = _modulated_norm_parts(x_ref, mod_ref)
    pos = lax.broadcasted_iota(jnp.int32, (tb, 1), 0) & (seq_len - 1)
    pb = _project(h, w_in_ref, B_COLS)
    pa = _project(h, w_in_ref, A_COLS)
    yb = _gmlp_branch(pb, gln_g_ref, gln_b_ref, ws_ref, bs_ref)
    pc = _project(h, w_in_ref, C_COLS)
    ya = _conv_branch(pa, None, pos, seq_len, conv_w_ref)

    q, k = pc[:, 0:C_W], pc[:, C_W:2 * C_W]
    v, gc = pc[:, 2 * C_W:3 * C_W], pc[:, 3 * C_W:4 * C_W]
    qs = q * (ATTN_SCALE * LOG2_E)
    gate = mod_ref[0, :, 2 * D_MODEL:3 * D_MODEL]
    row_half = lax.shift_right_logical(lax.broadcasted_iota(jnp.int32, (LANES, seq_len), 0), 6)
    ones_rows = jnp.ones((16, seq_len), BF16)

    for s in range(n_seq):
        rows = slice(s * seq_len, (s + 1) * seq_len)
        ko_ref[s, 0] = k[rows, :]
        vo_ref[s, 0] = v[rows, :]
        for later in range(1, ko_ref.shape[1]):
            ko_ref[s, later] = jnp.zeros((seq_len, C_W), F32)
            vo_ref[s, later] = jnp.zeros((seq_len, C_W), F32)

    def scores(s, head):
        rows = slice(s * seq_len, (s + 1) * seq_len)
        lanes = slice((head // 2) * LANES, (head // 2 + 1) * LANES)
        k_t = k[rows, lanes].astype(BF16)
        qt = qs[rows, lanes].T.astype(BF16)
        qt_h = jnp.where(row_half == head % 2, qt, jnp.zeros_like(qt))
        return jnp.dot(k_t, qt_h, preferred_element_type=F32)

    def values(s, head, sc):
        rows = slice(s * seq_len, (s + 1) * seq_len)
        lanes = slice((head // 2) * LANES, (head // 2 + 1) * LANES)
        vt = v[rows, lanes].T.astype(BF16)
        p = jnp.exp2(sc - jnp.max(sc, axis=0, keepdims=True)).astype(BF16)
        lhs = jnp.concatenate([vt[(head % 2) * C_HD:(head % 2 + 1) * C_HD, :], ones_rows], axis=0)
        o_t = jnp.dot(lhs, p, preferred_element_type=F32)
        return o_t[0:C_HD] / o_t[C_HD:C_HD + 1]

    def output(s, head_out):
        rows = slice(s * seq_len, (s + 1) * seq_len)
        attn = jnp.concatenate(
            [jnp.concatenate(head_out[2 * t:2 * t + 2], axis=0).T for t in range(N_HEAD_PAIRS)],
            axis=1)
        yc = attn * _silu(gc[rows, :])
        y = jnp.concatenate([ya[rows, :], yb[rows, :], yc], axis=1).astype(BF16)
        xo_ref[rows, :] = _post_norm(x_ref[rows, :], y, gate, w_out_ref, ln_g_ref, ln_b_ref)

    units = [(s, head) for s in range(n_seq) for head in range(C_HEADS)]
    pending, head_out = {}, {s: [] for s in range(n_seq)}
    for step in range(len(units) + SCORE_LEAD):
        if step < len(units):
            pending[units[step]] = scores(*units[step])
        if step >= SCORE_LEAD:
            s, head = units[step - SCORE_LEAD]
            head_out[s].append(values(s, head, pending.pop((s, head))))
            if head == C_HEADS - 1:
                output(s, head_out[s])


def _const_spec(shape):
    return pl.BlockSpec(shape, lambda *_: (0,) * len(shape), pipeline_mode=pl.Buffered(1))


def _layer_spec(shape, layer):
    return pl.BlockSpec((None, *shape), lambda *_: (layer,) + (0,) * len(shape),
                        pipeline_mode=pl.Buffered(1))


def _weight_specs(layer):
    return [
        _layer_spec((D_MODEL, IN_W), layer),
        _const_spec((3, A_W)),
        _const_spec((1, B_W)),
        _const_spec((1, B_W)),
        _const_spec((B_W // LANES, CHUNK, 2 * CHUNK)),
        _const_spec((B_W // LANES, CHUNK, LANES)),
    ]


def _ctx_layer(layer, xp, mod, lw, kv_prev):
    n_tok = xp.shape[0]
    seq_len = 256
    n_seq_total = n_tok // seq_len
    kv_shape = jax.ShapeDtypeStruct((n_seq_total, DEPTH, seq_len, C_W), F32)
    block = TOKEN_BLOCK if kv_prev is None else CTX_TOKEN_BLOCK
    if kv_prev is None:
        kv_spec = pl.BlockSpec((block // seq_len, DEPTH, seq_len, C_W), lambda i: (i, 0, 0, 0))
    else:
        kv_spec = pl.BlockSpec((block // seq_len, 1, seq_len, C_W), lambda i: (i, layer, 0, 0))
    in_specs = [
        pl.BlockSpec((block, D_MODEL), lambda i: (i, 0)),
        _const_spec((1, 1, 3 * D_MODEL)),
        *_weight_specs(layer),
        _layer_spec((D_MODEL, D_MODEL), layer),
        _const_spec((1, D_MODEL)),
        _const_spec((1, D_MODEL)),
    ]
    args = [xp, mod, lw["w_in"], lw["conv_w"], lw["gln_g"], lw["gln_b"], lw["ws"], lw["bs"],
            lw["w_out"], lw["ln_g"], lw["ln_b"]]
    aliases = {}
    if kv_prev is not None:
        in_specs += [pl.BlockSpec(memory_space=pl.ANY), pl.BlockSpec(memory_space=pl.ANY)]
        aliases = {len(args): 1, len(args) + 1: 2}
        args += list(kv_prev)
    return pl.pallas_call(
        functools.partial(_ctx_layer_kernel, seq_len),
        grid=(n_tok // block,),
        in_specs=in_specs,
        out_specs=[pl.BlockSpec((block, D_MODEL), lambda i: (i, 0)), kv_spec, kv_spec],
        out_shape=[jax.ShapeDtypeStruct((n_tok, D_MODEL), F32), kv_shape, kv_shape],
        input_output_aliases=aliases,
        compiler_params=pltpu.CompilerParams(
            dimension_semantics=("arbitrary",), vmem_limit_bytes=VMEM_LIMIT_BYTES),
        name=f"ctx_layer_{layer}",
    )(*args)


def _lat_proj_kernel(seq_len, x_ref, xprev_ref, xnext_ref, mod_ref, w_in_ref, conv_w_ref, gln_g_ref,
                     gln_b_ref, ws_ref, bs_ref, yab_ref, qt_ref, k_ref, vt_ref, g_ref):
    tb = x_ref.shape[0]
    group_tokens = GROUP_ROWS * GRID_W
    h = _modulated_norm_parts(x_ref, mod_ref)
    h_halo = _modulated_norm(jnp.concatenate([xprev_ref[0], xnext_ref[0]], axis=0), mod_ref)
    pos = (pl.program_id(0) * tb + lax.broadcasted_iota(jnp.int32, (tb, 1), 0)) & (seq_len - 1)
    z_halo = _conv_halo(h_halo, w_in_ref)
    pb = _project(h, w_in_ref, B_COLS)
    pa = _project(h, w_in_ref, A_COLS)
    yb = _gmlp_branch(pb, gln_g_ref, gln_b_ref, ws_ref, bs_ref)
    pvg = _project(h, w_in_ref, (C_COLS[0] + 2 * C_W, C_COLS[1]))
    ya = _conv_branch(pa, z_halo, pos, seq_len, conv_w_ref)
    yab_ref[...] = jnp.concatenate([ya, yb], axis=1).astype(BF16)
    pq = _project(h, w_in_ref, (C_COLS[0], C_COLS[0] + C_W))
    v = pvg[:, 0:C_W]
    for t in range(N_HEAD_PAIRS):
        for n in range(tb // LANES):
            vt_ref[t, n] = v[n * LANES:(n + 1) * LANES, t * LANES:(t + 1) * LANES].T.astype(BF16)
    g_ref[...] = _silu(pvg[:, C_W:2 * C_W]).astype(BF16)
    k = _project(h, w_in_ref, (C_COLS[0] + C_W, C_COLS[0] + 2 * C_W))
    qs = pq * (ATTN_SCALE * LOG2_E)
    for t in range(N_HEAD_PAIRS):
        for n in range(tb // group_tokens):
            qt_ref[t, n] = qs[n * group_tokens:(n + 1) * group_tokens,
                              t * LANES:(t + 1) * LANES].T.astype(BF16)
    for t in range(N_HEAD_PAIRS):
        k_ref[t] = k[:, t * LANES:(t + 1) * LANES].astype(BF16)


def _lat_proj(layer, xs, mod, lw, seq_len):
    n_tok = xs.shape[0]
    block = PROJ_TOKEN_BLOCK
    blocks_per_seq = seq_len // block
    n_halo = n_tok // 8
    halo_per_block = block // 8
    xs8 = xs.reshape(n_halo, 8, D_MODEL)
    group_tokens = GROUP_ROWS * GRID_W
    pair_shape = jax.ShapeDtypeStruct((N_HEAD_PAIRS, n_tok, LANES), BF16)
    pair_spec = pl.BlockSpec((N_HEAD_PAIRS, block, LANES), lambda i: (0, i, 0))
    return pl.pallas_call(
        functools.partial(_lat_proj_kernel, seq_len),
        grid=(n_tok // block,),
        in_specs=[
            pl.BlockSpec((block, D_MODEL), lambda i: (i, 0)),
            pl.BlockSpec((1, 8, D_MODEL), lambda i: (jnp.maximum(i * halo_per_block - 1, 0), 0, 0)),
            pl.BlockSpec((1, 8, D_MODEL),
                         lambda i: (jnp.minimum((i + 1) * halo_per_block, n_halo - 1), 0, 0)),
            pl.BlockSpec((1, 1, 3 * D_MODEL), lambda i: (i // blocks_per_seq, 0, 0)),
            *_weight_specs(layer),
        ],
        out_specs=[
            pl.BlockSpec((block, A_W + B_W), lambda i: (i, 0)),
            pl.BlockSpec((N_HEAD_PAIRS, block // group_tokens, LANES, group_tokens),
                         lambda i: (0, i, 0, 0)),
            pair_spec,
            pl.BlockSpec((N_HEAD_PAIRS, block // LANES, LANES, LANES), lambda i: (0, i, 0, 0)),
            pl.BlockSpec((block, C_W), lambda i: (i, 0)),
        ],
        out_shape=[
            jax.ShapeDtypeStruct((n_tok, A_W + B_W), BF16),
            jax.ShapeDtypeStruct((N_HEAD_PAIRS, n_tok // group_tokens, LANES, group_tokens), BF16),
            pair_shape,
            jax.ShapeDtypeStruct((N_HEAD_PAIRS, n_tok // LANES, LANES, LANES), BF16),
            jax.ShapeDtypeStruct((n_tok, C_W), BF16),
        ],
        compiler_params=pltpu.CompilerParams(
            dimension_semantics=("arbitrary",), vmem_limit_bytes=VMEM_LIMIT_BYTES),
        name=f"lat_proj_{layer}",
    )(xs, xs8, xs8, mod, lw["w_in"], lw["conv_w"], lw["gln_g"], lw["gln_b"], lw["ws"], lw["bs"])


GROUP_INTERIOR, GROUP_FIRST, GROUP_LAST = 0, 1, 2


def _bias_entry_table():
    table = np.zeros((3, WINDOW_ROWS * (GROUP_ROWS // 2)), np.int32)
    for kind, base in ((GROUP_INTERIOR, 3), (GROUP_FIRST, 7), (GROUP_LAST, -1)):
        for j in range(WINDOW_ROWS):
            for ip in range(GROUP_ROWS // 2):
                e = base + j - 2 * ip
                if kind == GROUP_INTERIOR:
                    entry = (ENTRY_LOW_ONLY if e == 3 else ENTRY_HIGH_ONLY if e == 11
                             else ENTRY_MASKED if (e < 3 or e > 11) else e)
                elif kind == GROUP_FIRST:
                    entry = ENTRY_MASKED if j >= WIN_ROWS else e
                else:
                    entry = ENTRY_MASKED if j < WINDOW_ROWS - WIN_ROWS else e
                assert 0 <= entry < N_BIAS_ENTRIES
                table[kind, j * (GROUP_ROWS // 2) + ip] = entry
    return table


def _lat_attn_kernel(ent_ref, x_ref, yab_ref, qt_ref, g_ref, k_ref, vt_ref, ck_ref, cvt_ref, tbl_ref,
                     mod_ref, w_out_ref, ln_g_ref, ln_b_ref, xo_ref, s_scr, m_scr, ot_scr):
    rb = pl.program_id(1)
    tb = x_ref.shape[0]
    group_tokens = GROUP_ROWS * GRID_W
    window_tokens = WINDOW_ROWS * GRID_W
    window_tiles = window_tokens // LANES
    n_groups = (k_ref.shape[1] // GRID_W) // GROUP_ROWS
    groups_per_block = tb // group_tokens
    n_units = N_HEAD_PAIRS * groups_per_block * 2
    group_bits = groups_per_block.bit_length() - 1
    assert groups_per_block == 1 << group_bits

    def coords(u):
        hp = lax.shift_right_logical(u, 1 + group_bits)
        gi = lax.shift_right_logical(u, 1) & (groups_per_block - 1)
        parity = u & 1
        g = rb * groups_per_block + gi
        first_tile = jnp.clip(2 * g - 2, 0, (GRID_W - WINDOW_ROWS) // 2)
        return hp, gi, parity, g, first_tile

    def score_stage(u, slot):
        hp, gi, parity, g, first_tile = coords(u)
        start = pl.multiple_of(first_tile * LANES, LANES)
        kw = k_ref[hp, pl.ds(start, window_tokens), :]
        qt = qt_ref[hp, gi]
        row_half = lax.shift_right_logical(
            lax.broadcasted_iota(jnp.int32, (LANES, group_tokens), 0), 6)
        qt_h = jnp.where(row_half == parity, qt, jnp.zeros_like(qt))
        key_chunk = 256
        s_loc = jnp.concatenate(
            [jnp.dot(kw[c:c + key_chunk], qt_h, preferred_element_type=F32)
             for c in range(0, window_tokens, key_chunk)], axis=0)
        s_ctx = jnp.dot(ck_ref[hp, 0], qt_h, preferred_element_type=F32)
        head_base = (2 * hp + parity) * N_BIAS_ENTRIES
        kind = jnp.where(g == 0, GROUP_FIRST, jnp.where(g == n_groups - 1, GROUP_LAST, GROUP_INTERIOR))
        bias_rows = []
        for j in range(WINDOW_ROWS):
            tiles = []
            for ip in range(GROUP_ROWS // 2):
                tiles.append(tbl_ref[head_base + ent_ref[kind, j * (GROUP_ROWS // 2) + ip]])
            bias_rows.append(jnp.concatenate(tiles, axis=1))
        s_loc = s_loc + jnp.concatenate(bias_rows, axis=0)
        s_scr[slot, 0:window_tokens, :] = s_loc
        s_scr[slot, window_tokens:, :] = s_ctx
        m_scr[slot] = jnp.maximum(jnp.max(s_loc, axis=0, keepdims=True),
                                  jnp.max(s_ctx, axis=0, keepdims=True))

    def value_stage(u, slot):
        hp, gi, parity, g, first_tile = coords(u)
        p = jnp.exp2(s_scr[slot] - m_scr[slot]).astype(BF16)
        rows = pl.ds(pl.multiple_of(parity * C_HD, C_HD), C_HD)
        vt_tiles = vt_ref[hp, pl.ds(first_tile, window_tiles), rows, :]
        vtw = jnp.concatenate([vt_tiles[n] for n in range(window_tiles)] + [cvt_ref[hp, 0, rows, :]],
                              axis=1)
        lhs = jnp.concatenate([vtw, jnp.ones((16, s_scr.shape[1]), BF16)], axis=0)
        o_t = jnp.dot(lhs, p, preferred_element_type=F32)
        ot_scr[hp, gi, rows, :] = o_t[0:C_HD] / o_t[C_HD:C_HD + 1]

    n_slots = s_scr.shape[0]
    assert n_slots > SCORE_LEAD_LAT and n_units % n_slots == 0

    def pipeline_step(j, carry):
        for i in range(n_slots):
            score_stage(n_slots * j + i + SCORE_LEAD_LAT, (i + SCORE_LEAD_LAT) % n_slots)
            value_stage(n_slots * j + i, i)
        return carry

    gate = mod_ref[0, :, 2 * D_MODEL:3 * D_MODEL]

    def output_stage(gi):
        rows = slice(gi * group_tokens, (gi + 1) * group_tokens)
        attn = jnp.concatenate([ot_scr[t, gi].T for t in range(N_HEAD_PAIRS)], axis=1)
        yc = (attn * g_ref[rows, :].astype(F32)).astype(BF16)
        y = jnp.concatenate([yab_ref[rows, :], yc], axis=1)
        xo_ref[rows, :] = _post_norm(x_ref[rows, :], y, gate, w_out_ref, ln_g_ref, ln_b_ref)

    n_tail = 2 * groups_per_block
    assert (n_units - n_tail) % n_slots == 0
    for u in range(SCORE_LEAD_LAT):
        score_stage(jnp.int32(u), u)
    lax.fori_loop(0, (n_units - n_tail) // n_slots, pipeline_step, 0)
    for u in range(n_units - n_tail, n_units):
        if u + SCORE_LEAD_LAT < n_units:
            score_stage(jnp.int32(u + SCORE_LEAD_LAT), (u + SCORE_LEAD_LAT) % n_slots)
        value_stage(jnp.int32(u), u % n_slots)
        if u % 2 == 1:
            output_stage((u - (n_units - n_tail)) // 2)


def _lat_attn(layer, xs, yab, q, k, vt, gc, ck, cvt, tbl, mod, lw, seq_len):
    n_tok = xs.shape[0]
    n_batch = n_tok // seq_len
    blocks_per_seq = seq_len // ATTN_TOKEN_BLOCK
    past = ck.shape[2]
    group_tokens = GROUP_ROWS * GRID_W
    tok_spec = lambda width: pl.BlockSpec((ATTN_TOKEN_BLOCK, width),
                                          lambda b, r: (b * blocks_per_seq + r, 0))
    return pl.pallas_call(
        _lat_attn_kernel,
        grid=(n_batch, blocks_per_seq),
        in_specs=[
            pl.BlockSpec(memory_space=pltpu.SMEM),
            tok_spec(D_MODEL),
            tok_spec(A_W + B_W),
            pl.BlockSpec((N_HEAD_PAIRS, ATTN_TOKEN_BLOCK // group_tokens, LANES, group_tokens),
                         lambda b, r: (0, b * blocks_per_seq + r, 0, 0)),
            tok_spec(C_W),
            pl.BlockSpec((N_HEAD_PAIRS, seq_len, LANES), lambda b, r: (0, b, 0)),
            pl.BlockSpec((N_HEAD_PAIRS, seq_len // LANES, LANES, LANES), lambda b, r: (0, b, 0, 0)),
            pl.BlockSpec((N_HEAD_PAIRS, 1, past, LANES), lambda b, r: (0, b, 0, 0)),
            pl.BlockSpec((N_HEAD_PAIRS, 1, LANES, past), lambda b, r: (0, b, 0, 0)),
            _layer_spec(tbl.shape[1:], layer),
            pl.BlockSpec((1, 1, 3 * D_MODEL), lambda b, r: (b, 0, 0)),
            _layer_spec((D_MODEL, D_MODEL), layer),
            _const_spec((1, D_MODEL)),
            _const_spec((1, D_MODEL)),
        ],
        out_specs=tok_spec(D_MODEL),
        out_shape=jax.ShapeDtypeStruct((n_tok, D_MODEL), F32),
        scratch_shapes=[
            pltpu.VMEM((4, WINDOW_ROWS * GRID_W + past, group_tokens), F32),
            pltpu.VMEM((4, 1, group_tokens), F32),
            pltpu.VMEM((N_HEAD_PAIRS, ATTN_TOKEN_BLOCK // group_tokens, LANES, group_tokens), F32),
        ],
        compiler_params=pltpu.CompilerParams(
            dimension_semantics=("arbitrary", "arbitrary"),
            vmem_limit_bytes=VMEM_LIMIT_BYTES),
        name=f"lat_attn_{layer}",
    )(jnp.asarray(_bias_entry_table()), xs, yab, q, gc, k, vt, ck, cvt, tbl, mod, lw["w_out"],
      lw["ln_g"], lw["ln_b"])


def _bias_table_kernel(rp_ref, o_ref):
    n_taps = 2 * WIN_COLS - 1
    n_dr = 2 * WIN_ROWS - 1
    cp = lax.broadcasted_iota(jnp.int32, (GRID_W, LANES), 0)
    lane = lax.broadcasted_iota(jnp.int32, (GRID_W, LANES), 1)
    c = lane & (GRID_W - 1)
    cs = jnp.clip(c - WIN_COLS // 2, 0, GRID_W - WIN_COLS)
    valid = (cp >= cs) & (cp < cs + WIN_COLS)
    low = lane < GRID_W
    neg = jnp.full((GRID_W, LANES), NEG_INF, F32)

    def diagonal(head, e):
        row = jnp.broadcast_to(rp_ref[0, head, e:e + 1, :], (GRID_W, LANES))
        return pltpu.roll(row, LANES - n_taps // 2, 1, stride=1, stride_axis=0) * LOG2_E

    for head in range(C_HEADS):
        diag = [diagonal(head, e) for e in range(n_dr)]

        def pair(e_low, e_high):
            lo = neg if e_low is None else diag[e_low]
            hi = neg if e_high is None else diag[e_high]
            return jnp.where(valid, jnp.where(low, lo, hi), NEG_INF)

        for e in range(n_dr + 1):
            o_ref[0, head, e] = pair(e if e < n_dr else None, e - 1 if e >= 1 else None)
        o_ref[0, head, ENTRY_MASKED] = neg
        o_ref[0, head, ENTRY_LOW_ONLY] = pair(3, None)
        o_ref[0, head, ENTRY_HIGH_ONLY] = pair(None, 10)


def _bias_tables(rpb):
    n_dr, n_taps = rpb.shape[2], rpb.shape[3]
    rp = jnp.pad(rpb[..., ::-1], ((0, 0), (0, 0), (0, 16 - n_dr), (0, GRID_W - n_taps)),
                 constant_values=NEG_INF)
    rp = jnp.concatenate([rp, rp], axis=-1)
    tbl = pl.pallas_call(
        _bias_table_kernel,
        grid=(DEPTH,),
        in_specs=[pl.BlockSpec((1, C_HEADS, 16, LANES), lambda l: (l, 0, 0, 0))],
        out_specs=pl.BlockSpec((1, C_HEADS, N_BIAS_ENTRIES, GRID_W, LANES), lambda l: (l, 0, 0, 0, 0)),
        out_shape=jax.ShapeDtypeStruct((DEPTH, C_HEADS, N_BIAS_ENTRIES, GRID_W, LANES), F32),
        compiler_params=pltpu.CompilerParams(dimension_semantics=("arbitrary",)),
        name="bias_tables",
    )(rp)
    return tbl.reshape(DEPTH, C_HEADS * N_BIAS_ENTRIES, GRID_W, LANES)


def _layer_weights(l, w_in_bf, w_out_bf, conv_w, gmlp_ln_g, gmlp_ln_b, ws_pairs, bs_lanes, ln_g, ln_b):
    return {
        "w_in": w_in_bf, "w_out": w_out_bf, "conv_w": conv_w[l],
        "gln_g": gmlp_ln_g[l][None], "gln_b": gmlp_ln_b[l][None],
        "ws": ws_pairs[l], "bs": bs_lanes[l],
        "ln_g": ln_g[l][None], "ln_b": ln_b[l][None],
    }


def kernel(x_prompt, x_sample, cache_k, cache_v, c, c_ctx, w_ada, b_ada, w_in, conv_w, gmlp_ln_g,
           gmlp_ln_b, w_spatial, b_spatial, rpb, w_out, ln_g, ln_b):
    batch, seq, d = x_prompt.shape
    dec_batch, dec_seq, _ = x_sample.shape
    past = cache_k.shape[2]

    n_mod_rows = 8
    cvec = jnp.concatenate(
        [c_ctx[None], c, jnp.zeros((n_mod_rows - 1 - dec_batch, d), F32)], axis=0)
    mod = _modulation(cvec, w_ada, b_ada)

    w_in_bf = w_in.astype(BF16)
    w_out_bf = w_out.astype(BF16)
    ws_pairs = w_spatial.reshape(DEPTH, 2, 2, CHUNK, CHUNK).transpose(0, 1, 3, 2, 4)
    ws_pairs = ws_pairs.reshape(DEPTH, 2, CHUNK, 2 * CHUNK).astype(BF16)
    bs_lanes = jnp.repeat(b_spatial.reshape(DEPTH, 2, 2, CHUNK).transpose(0, 1, 3, 2), C_HD, axis=-1)

    def pair_major(cache):
        cache = cache.reshape(dec_batch, DEPTH, past, N_HEAD_PAIRS, LANES)
        return cache.transpose(1, 3, 0, 2, 4).astype(BF16)

    ck_all = pair_major(cache_k)
    cvt_all = jnp.swapaxes(pair_major(cache_v), -1, -2)
    tables = _bias_tables(rpb)

    xp = x_prompt.reshape(batch * seq, d)
    xs = x_sample.reshape(dec_batch * dec_seq, d)
    kv = None
    for l in range(DEPTH):
        lw = _layer_weights(l, w_in_bf, w_out_bf, conv_w, gmlp_ln_g, gmlp_ln_b, ws_pairs, bs_lanes,
                            ln_g, ln_b)
        xp, new_k, new_v = _ctx_layer(l, xp, mod[l, 0:1][None], lw, kv)
        kv = (new_k, new_v)
        mod_lat = mod[l, 1:1 + dec_batch][:, None]
        yab, q, k, vt, gc = _lat_proj(l, xs, mod_lat, lw, dec_seq)
        xs = _lat_attn(l, xs, yab, q, k, vt, gc, ck_all[l], cvt_all[l], tables, mod_lat, lw, dec_seq)

    new_k, new_v = kv
    return (xp.reshape(batch, seq, d), xs.reshape(dec_batch, dec_seq, d),
            new_k.reshape(batch, DEPTH, seq, C_HEADS, C_HD),
            new_v.reshape(batch, DEPTH, seq, C_HEADS, C_HD))
```

```python
import functools

import jax
import jax.numpy as jnp
import numpy as np
from jax import lax
from jax.experimental import pallas as pl
from jax.experimental.pallas import tpu as pltpu

F32 = jnp.float32
BF16 = jnp.bfloat16

D_MODEL = 1024
DEPTH = 4
GRID_W = 64
A_W = 256
B_W = 256
CHUNK = 128
C_W = 512
C_HEADS = 8
C_HD = 64
WIN_ROWS = 8
WIN_COLS = 16
IN_W = 3840
ALPHA = (2 * DEPTH) ** 0.25
ATTN_SCALE = C_HD ** -0.5
NEG_INF = -1e30
LN_EPS = 1e-5
LOG2_E = 1.4426950408889634

LANES = 128
N_HEAD_PAIRS = C_W // LANES
A_COLS = (0, 4 * A_W)
B_COLS = (4 * A_W, 4 * A_W + 3 * B_W)
C_COLS = (4 * A_W + 3 * B_W, IN_W)

TOKEN_BLOCK = 512
ATTN_TOKEN_BLOCK = 1024
PROJ_TOKEN_BLOCK = 2048
CTX_TOKEN_BLOCK = 1024
SLAB_ROWS = 256
SCORE_LEAD_LAT = 2
SCORE_LEAD = 8
GROUP_ROWS = 4
WINDOW_ROWS = 12
ENTRY_MASKED = 16
ENTRY_LOW_ONLY = 17
ENTRY_HIGH_ONLY = 18
N_BIAS_ENTRIES = 19
VMEM_LIMIT_BYTES = 58 * 1024 * 1024


def _layernorm(x):
    mu = jnp.mean(x, axis=-1, keepdims=True)
    xc = x - mu
    var = jnp.mean(xc * xc, axis=-1, keepdims=True)
    return xc * lax.rsqrt(var + LN_EPS)


def _silu(x):
    return x / (1.0 + jnp.exp(-x))


def _lane_is_low(shape):
    return lax.broadcasted_iota(jnp.int32, shape, len(shape) - 1) < C_HD


def _mod_kernel(c_ref, w_ref, b_ref, o_ref):
    s = _silu(c_ref[...])
    w = w_ref[0]
    s_hi, w_hi = s.astype(BF16), w.astype(BF16)
    s_lo = (s - s_hi.astype(F32)).astype(BF16)
    w_lo = (w - w_hi.astype(F32)).astype(BF16)
    o_ref[0] = (jnp.dot(s_hi, w_hi, preferred_element_type=F32)
                + (jnp.dot(s_hi, w_lo, preferred_element_type=F32)
                   + jnp.dot(s_lo, w_hi, preferred_element_type=F32))) + b_ref[0]


def _modulation(cvec, w_ada, b_ada):
    n_rows = cvec.shape[0]
    col_block = 1024
    return pl.pallas_call(
        _mod_kernel,
        grid=(DEPTH, 3 * D_MODEL // col_block),
        in_specs=[
            pl.BlockSpec((n_rows, D_MODEL), lambda l, j: (0, 0)),
            pl.BlockSpec((1, D_MODEL, col_block), lambda l, j: (l, 0, j)),
            pl.BlockSpec((1, 1, col_block), lambda l, j: (l, 0, j)),
        ],
        out_specs=pl.BlockSpec((1, n_rows, col_block), lambda l, j: (l, 0, j)),
        out_shape=jax.ShapeDtypeStruct((DEPTH, n_rows, 3 * D_MODEL), F32),
        compiler_params=pltpu.CompilerParams(
            dimension_semantics=("arbitrary", "arbitrary"), vmem_limit_bytes=VMEM_LIMIT_BYTES),
        name="adaln_modulation",
    )(cvec, w_ada, b_ada.reshape(DEPTH, 1, 3 * D_MODEL))


def _modulated_norm(x, mod_ref):
    shift = mod_ref[0, :, 0:D_MODEL]
    scale = mod_ref[0, :, D_MODEL:2 * D_MODEL]
    return (_layernorm(x) * (1.0 + scale) + shift).astype(BF16)


def _modulated_norm_parts(x_ref, mod_ref):
    return [_modulated_norm(x_ref[lo:lo + SLAB_ROWS, :], mod_ref)
            for lo in range(0, x_ref.shape[0], SLAB_ROWS)]


def _project(h_parts, w_in_ref, cols):
    return jnp.concatenate(
        [jnp.dot(h, w_in_ref[:, cols[0]:cols[1]], preferred_element_type=F32) for h in h_parts],
        axis=0)


def _conv_halo(h_halo, w_in_ref):
    xh = jnp.dot(h_halo, w_in_ref[:, 0:A_W], preferred_element_type=F32)
    ch = jnp.dot(h_halo, w_in_ref[:, 2 * A_W:3 * A_W], preferred_element_type=F32)
    return ch * xh


def _conv_branch(pa, z_halo, pos, seq_len, conv_w_ref):
    tb = pa.shape[0]
    xa, ba = pa[:, 0:A_W], pa[:, A_W:2 * A_W]
    ca, ga = pa[:, 2 * A_W:3 * A_W], pa[:, 3 * A_W:4 * A_W]
    z = ca * xa
    z_prev = pltpu.roll(z, 1, 0)
    z_next = pltpu.roll(z, tb - 1, 0)
    if z_halo is not None:
        row = lax.broadcasted_iota(jnp.int32, (tb, 1), 0)
        z_prev = jnp.where(row == 0, z_halo[7:8, :], z_prev)
        z_next = jnp.where(row == tb - 1, z_halo[8:9, :], z_next)
    z_prev = jnp.where(pos == 0, 0.0, z_prev)
    z_next = jnp.where(pos == seq_len - 1, 0.0, z_next)
    cw = conv_w_ref[...]
    conv = cw[0:1, :] * z_prev + cw[1:2, :] * z + cw[2:3, :] * z_next
    return ba * conv * _silu(ga)


def _gmlp_branch(pb, gln_g_ref, gln_b_ref, ws_ref, bs_ref):
    tb = pb.shape[0]
    u, v, gb = pb[:, 0:B_W], pb[:, B_W:2 * B_W], pb[:, 2 * B_W:3 * B_W]
    vn = _layernorm(v) * gln_g_ref[...] + gln_b_ref[...]
    low = _lane_is_low((CHUNK, LANES))
    chunks = []
    for c in range(tb // CHUNK):
        tiles = []
        for t in range(B_W // LANES):
            vt = vn[c * CHUNK:(c + 1) * CHUNK, t * LANES:(t + 1) * LANES]
            rhs = jnp.concatenate([jnp.where(low, vt, 0.0), jnp.where(low, 0.0, vt)],
                                  axis=0).astype(BF16)
            tiles.append(jnp.dot(ws_ref[t], rhs, preferred_element_type=F32) + bs_ref[t])
        chunks.append(jnp.concatenate(tiles, axis=1))
    sv = jnp.concatenate(chunks, axis=0)
    return u * sv * _silu(gb)


def _post_norm(x, y_bf16, gate, w_out_ref, ln_g_ref, ln_b_ref):
    out = jnp.dot(y_bf16, w_out_ref[...], preferred_element_type=F32)
    return _layernorm(ALPHA * x + gate * out) * ln_g_ref[...] + ln_b_ref[...]


def _ctx_layer_kernel(seq_len, x_ref, mod_ref, w_in_ref, conv_w_ref, gln_g_ref, gln_b_ref, ws_ref,
                      bs_ref, w_out_ref, ln_g_ref, ln_b_ref, *rest):
    xo_ref, ko_ref, vo_ref = rest[-3:]
    tb = x_ref.shape[0]
    n_seq = tb // seq_len
    h = _modulated_norm_parts(x_ref, mod_ref)
    pos = lax.broadcasted_iota(jnp.int32, (tb, 1), 0) & (seq_len - 1)
    pb = _project(h, w_in_ref, B_COLS)
    pa = _project(h, w_in_ref, A_COLS)
    yb = _gmlp_branch(pb, gln_g_ref, gln_b_ref, ws_ref, bs_ref)
    pc = _project(h, w_in_ref, C_COLS)
    ya = _conv_branch(pa, None, pos, seq_len, conv_w_ref)

    q, k = pc[:, 0:C_W], pc[:, C_W:2 * C_W]
    v, gc = pc[:, 2 * C_W:3 * C_W], pc[:, 3 * C_W:4 * C_W]
    qs = q * (ATTN_SCALE * LOG2_E)
    gate = mod_ref[0, :, 2 * D_MODEL:3 * D_MODEL]
    row_half = lax.shift_right_logical(lax.broadcasted_iota(jnp.int32, (LANES, seq_len), 0), 6)
    ones_rows = jnp.ones((16, seq_len), BF16)

    for s in range(n_seq):
        rows = slice(s * seq_len, (s + 1) * seq_len)
        ko_ref[s, 0] = k[rows, :]
        vo_ref[s, 0] = v[rows, :]
        for later in range(1, ko_ref.shape[1]):
            ko_ref[s, later] = jnp.zeros((seq_len, C_W), F32)
            vo_ref[s, later] = jnp.zeros((seq_len, C_W), F32)

    def scores(s, head):
        rows = slice(s * seq_len, (s + 1) * seq_len)
        lanes = slice((head // 2) * LANES, (head // 2 + 1) * LANES)
        k_t = k[rows, lanes].astype(BF16)
        qt = qs[rows, lanes].T.astype(BF16)
        qt_h = jnp.where(row_half == head % 2, qt, jnp.zeros_like(qt))
        return jnp.dot(k_t, qt_h, preferred_element_type=F32)

    def values(s, head, sc):
        rows = slice(s * seq_len, (s + 1) * seq_len)
        lanes = slice((head // 2) * LANES, (head // 2 + 1) * LANES)
        vt = v[rows, lanes].T.astype(BF16)
        p = jnp.exp2(sc - jnp.max(sc, axis=0, keepdims=True)).astype(BF16)
        lhs = jnp.concatenate([vt[(head % 2) * C_HD:(head % 2 + 1) * C_HD, :], ones_rows], axis=0)
        o_t = jnp.dot(lhs, p, preferred_element_type=F32)
        return o_t[0:C_HD] / o_t[C_HD:C_HD + 1]

    def output(s, head_out):
        rows = slice(s * seq_len, (s + 1) * seq_len)
        attn = jnp.concatenate(
            [jnp.concatenate(head_out[2 * t:2 * t + 2], axis=0).T for t in range(N_HEAD_PAIRS)],
            axis=1)
        yc = attn * _silu(gc[rows, :])
        y = jnp.concatenate([ya[rows, :], yb[rows, :], yc], axis=1).astype(BF16)
        xo_ref[rows, :] = _post_norm(x_ref[rows, :], y, gate, w_out_ref, ln_g_ref, ln_b_ref)

    units = [(s, head) for s in range(n_seq) for head in range(C_HEADS)]
    pending, head_out = {}, {s: [] for s in range(n_seq)}
    for step in range(len(units) + SCORE_LEAD):
        if step < len(units):
            pending[units[step]] = scores(*units[step])
        if step >= SCORE_LEAD:
            s, head = units[step - SCORE_LEAD]
            head_out[s].append(values(s, head, pending.pop((s, head))))
            if head == C_HEADS - 1:
                output(s, head_out[s])


def _const_spec(shape):
    return pl.BlockSpec(shape, lambda *_: (0,) * len(shape), pipeline_mode=pl.Buffered(1))


def _layer_spec(shape, layer):
    return pl.BlockSpec((None, *shape), lambda *_: (layer,) + (0,) * len(shape),
                        pipeline_mode=pl.Buffered(1))


def _weight_specs(layer):
    return [
        _layer_spec((D_MODEL, IN_W), layer),
        _const_spec((3, A_W)),
        _const_spec((1, B_W)),
        _const_spec((1, B_W)),
        _const_spec((B_W // LANES, CHUNK, 2 * CHUNK)),
        _const_spec((B_W // LANES, CHUNK, LANES)),
    ]


def _ctx_layer(layer, xp, mod, lw, kv_prev):
    n_tok = xp.shape[0]
    seq_len = 256
    n_seq_total = n_tok // seq_len
    kv_shape = jax.ShapeDtypeStruct((n_seq_total, DEPTH, seq_len, C_W), F32)
    block = TOKEN_BLOCK if kv_prev is None else CTX_TOKEN_BLOCK
    if kv_prev is None:
        kv_spec = pl.BlockSpec((block // seq_len, DEPTH, seq_len, C_W), lambda i: (i, 0, 0, 0))
    else:
        kv_spec = pl.BlockSpec((block // seq_len, 1, seq_len, C_W), lambda i: (i, layer, 0, 0))
    in_specs = [
        pl.BlockSpec((block, D_MODEL), lambda i: (i, 0)),
        _const_spec((1, 1, 3 * D_MODEL)),
        *_weight_specs(layer),
        _layer_spec((D_MODEL, D_MODEL), layer),
        _const_spec((1, D_MODEL)),
        _const_spec((1, D_MODEL)),
    ]
    args = [xp, mod, lw["w_in"], lw["conv_w"], lw["gln_g"], lw["gln_b"], lw["ws"], lw["bs"],
            lw["w_out"], lw["ln_g"], lw["ln_b"]]
    aliases = {}
    if kv_prev is not None:
        in_specs += [pl.BlockSpec(memory_space=pl.ANY), pl.BlockSpec(memory_space=pl.ANY)]
        aliases = {len(args): 1, len(args) + 1: 2}
        args += list(kv_prev)
    return pl.pallas_call(
        functools.partial(_ctx_layer_kernel, seq_len),
        grid=(n_tok // block,),
        in_specs=in_specs,
        out_specs=[pl.BlockSpec((block, D_MODEL), lambda i: (i, 0)), kv_spec, kv_spec],
        out_shape=[jax.ShapeDtypeStruct((n_tok, D_MODEL), F32), kv_shape, kv_shape],
        input_output_aliases=aliases,
        compiler_params=pltpu.CompilerParams(
            dimension_semantics=("arbitrary",), vmem_limit_bytes=VMEM_LIMIT_BYTES),
        name=f"ctx_layer_{layer}",
    )(*args)


def _lat_proj_kernel(seq_len, x_ref, xprev_ref, xnext_ref, mod_ref, w_in_ref, conv_w_ref, gln_g_ref,
                     gln_b_ref, ws_ref, bs_ref, yab_ref, qt_ref, k_ref, vt_ref, g_ref):
    tb = x_ref.shape[0]
    group_tokens = GROUP_ROWS * GRID_W
    h = _modulated_norm_parts(x_ref, mod_ref)
    h_halo = _modulated_norm(jnp.concatenate([xprev_ref[0], xnext_ref[0]], axis=0), mod_ref)
    pos = (pl.program_id(0) * tb + lax.broadcasted_iota(jnp.int32, (tb, 1), 0)) & (seq_len - 1)
    z_halo = _conv_halo(h_halo, w_in_ref)
    pb = _project(h, w_in_ref, B_COLS)
    pa = _project(h, w_in_ref, A_COLS)
    yb = _gmlp_branch(pb, gln_g_ref, gln_b_ref, ws_ref, bs_ref)
    pvg = _project(h, w_in_ref, (C_COLS[0] + 2 * C_W, C_COLS[1]))
    ya = _conv_branch(pa, z_halo, pos, seq_len, conv_w_ref)
    yab_ref[...] = jnp.concatenate([ya, yb], axis=1).astype(BF16)
    pq = _project(h, w_in_ref, (C_COLS[0], C_COLS[0] + C_W))
    v = pvg[:, 0:C_W]
    for t in range(N_HEAD_PAIRS):
        for n in range(tb // LANES):
            vt_ref[t, n] = v[n * LANES:(n + 1) * LANES, t * LANES:(t + 1) * LANES].T.astype(BF16)
    g_ref[...] = _silu(pvg[:, C_W:2 * C_W]).astype(BF16)
    k = _project(h, w_in_ref, (C_COLS[0] + C_W, C_COLS[0] + 2 * C_W))
    qs = pq * (ATTN_SCALE * LOG2_E)
    for t in range(N_HEAD_PAIRS):
        for n in range(tb // group_tokens):
            qt_ref[t, n] = qs[n * group_tokens:(n + 1) * group_tokens,
                              t * LANES:(t + 1) * LANES].T.astype(BF16)
    for t in range(N_HEAD_PAIRS):
        k_ref[t] = k[:, t * LANES:(t + 1) * LANES].astype(BF16)


def _lat_proj(layer, xs, mod, lw, seq_len):
    n_tok = xs.shape[0]
    block = PROJ_TOKEN_BLOCK
    blocks_per_seq = seq_len // block
    n_halo = n_tok // 8
    halo_per_block = block // 8
    xs8 = xs.reshape(n_halo, 8, D_MODEL)
    group_tokens = GROUP_ROWS * GRID_W
    pair_shape = jax.ShapeDtypeStruct((N_HEAD_PAIRS, n_tok, LANES), BF16)
    pair_spec = pl.BlockSpec((N_HEAD_PAIRS, block, LANES), lambda i: (0, i, 0))
    return pl.pallas_call(
        functools.partial(_lat_proj_kernel, seq_len),
        grid=(n_tok // block,),
        in_specs=[
            pl.BlockSpec((block, D_MODEL), lambda i: (i, 0)),
            pl.BlockSpec((1, 8, D_MODEL), lambda i: (jnp.maximum(i * halo_per_block - 1, 0), 0, 0)),
            pl.BlockSpec((1, 8, D_MODEL),
                         lambda i: (jnp.minimum((i + 1) * halo_per_block, n_halo - 1), 0, 0)),
            pl.BlockSpec((1, 1, 3 * D_MODEL), lambda i: (i // blocks_per_seq, 0, 0)),
            *_weight_specs(layer),
        ],
        out_specs=[
            pl.BlockSpec((block, A_W + B_W), lambda i: (i, 0)),
            pl.BlockSpec((N_HEAD_PAIRS, block // group_tokens, LANES, group_tokens),
                         lambda i: (0, i, 0, 0)),
            pair_spec,
            pl.BlockSpec((N_HEAD_PAIRS, block // LANES, LANES, LANES), lambda i: (0, i, 0, 0)),
            pl.BlockSpec((block, C_W), lambda i: (i, 0)),
        ],
        out_shape=[
            jax.ShapeDtypeStruct((n_tok, A_W + B_W), BF16),
            jax.ShapeDtypeStruct((N_HEAD_PAIRS, n_tok // group_tokens, LANES, group_tokens), BF16),
            pair_shape,
            jax.ShapeDtypeStruct((N_HEAD_PAIRS, n_tok // LANES, LANES, LANES), BF16),
            jax.ShapeDtypeStruct((n_tok, C_W), BF16),
        ],
        compiler_params=pltpu.CompilerParams(
            dimension_semantics=("arbitrary",), vmem_limit_bytes=VMEM_LIMIT_BYTES),
        name=f"lat_proj_{layer}",
    )(xs, xs8, xs8, mod, lw["w_in"], lw["conv_w"], lw["gln_g"], lw["gln_b"], lw["ws"], lw["bs"])


GROUP_INTERIOR, GROUP_FIRST, GROUP_LAST = 0, 1, 2


def _bias_entry_table():
    table = np.zeros((3, WINDOW_ROWS * (GROUP_ROWS // 2)), np.int32)
    for kind, base in ((GROUP_INTERIOR, 3), (GROUP_FIRST, 7), (GROUP_LAST, -1)):
        for j in range(WINDOW_ROWS):
            for ip in range(GROUP_ROWS // 2):
                e = base + j - 2 * ip
                if kind == GROUP_INTERIOR:
                    entry = (ENTRY_LOW_ONLY if e == 3 else ENTRY_HIGH_ONLY if e == 11
                             else ENTRY_MASKED if (e < 3 or e > 11) else e)
                elif kind == GROUP_FIRST:
                    entry = ENTRY_MASKED if j >= WIN_ROWS else e
                else:
                    entry = ENTRY_MASKED if j < WINDOW_ROWS - WIN_ROWS else e
                assert 0 <= entry < N_BIAS_ENTRIES
                table[kind, j * (GROUP_ROWS // 2) + ip] = entry
    return table


def _lat_attn_kernel(ent_ref, x_ref, yab_ref, qt_ref, g_ref, k_ref, vt_ref, cache_k_ref, cache_v_ref,
                     tbl_ref, mod_ref, w_out_ref, ln_g_ref, ln_b_ref, xo_ref, s_scr, m_scr, ot_scr,
                     ck_scr, cvt_scr):
    rb = pl.program_id(1)

    @pl.when(rb == 0)
    def _():
        for t in range(N_HEAD_PAIRS):
            lanes = slice(t * LANES, (t + 1) * LANES)
            ck_scr[t] = cache_k_ref[:, lanes].astype(BF16)
            cvt_scr[t] = cache_v_ref[:, lanes].T.astype(BF16)
    tb = x_ref.shape[0]
    group_tokens = GROUP_ROWS * GRID_W
    window_tokens = WINDOW_ROWS * GRID_W
    window_tiles = window_tokens // LANES
    n_groups = (k_ref.shape[1] // GRID_W) // GROUP_ROWS
    groups_per_block = tb // group_tokens
    n_units = N_HEAD_PAIRS * groups_per_block * 2
    group_bits = groups_per_block.bit_length() - 1
    assert groups_per_block == 1 << group_bits

    def coords(u):
        hp = lax.shift_right_logical(u, 1 + group_bits)
        gi = lax.shift_right_logical(u, 1) & (groups_per_block - 1)
        parity = u & 1
        g = rb * groups_per_block + gi
        first_tile = jnp.clip(2 * g - 2, 0, (GRID_W - WINDOW_ROWS) // 2)
        return hp, gi, parity, g, first_tile

    def score_stage(u, slot):
        hp, gi, parity, g, first_tile = coords(u)
        start = pl.multiple_of(first_tile * LANES, LANES)
        kw = k_ref[hp, pl.ds(start, window_tokens), :]
        qt = qt_ref[hp, gi]
        row_half = lax.shift_right_logical(
            lax.broadcasted_iota(jnp.int32, (LANES, group_tokens), 0), 6)
        qt_h = jnp.where(row_half == parity, qt, jnp.zeros_like(qt))
        key_chunk = 256
        s_loc = jnp.concatenate(
            [jnp.dot(kw[c:c + key_chunk], qt_h, preferred_element_type=F32)
             for c in range(0, window_tokens, key_chunk)], axis=0)
        s_ctx = jnp.dot(ck_scr[hp], qt_h, preferred_element_type=F32)
        head_base = (2 * hp + parity) * N_BIAS_ENTRIES
        kind = jnp.where(g == 0, GROUP_FIRST, jnp.where(g == n_groups - 1, GROUP_LAST, GROUP_INTERIOR))
        bias_rows = []
        for j in range(WINDOW_ROWS):
            tiles = []
            for ip in range(GROUP_ROWS // 2):
                tiles.append(tbl_ref[head_base + ent_ref[kind, j * (GROUP_ROWS // 2) + ip]])
            bias_rows.append(jnp.concatenate(tiles, axis=1))
        s_loc = s_loc + jnp.concatenate(bias_rows, axis=0)
        s_scr[slot, 0:window_tokens, :] = s_loc
        s_scr[slot, window_tokens:, :] = s_ctx
        m_scr[slot] = jnp.maximum(jnp.max(s_loc, axis=0, keepdims=True),
                                  jnp.max(s_ctx, axis=0, keepdims=True))

    def value_stage(u, slot):
        hp, gi, parity, g, first_tile = coords(u)
        p = jnp.exp2(s_scr[slot] - m_scr[slot]).astype(BF16)
        rows = pl.ds(pl.multiple_of(parity * C_HD, C_HD), C_HD)
        vt_tiles = vt_ref[hp, pl.ds(first_tile, window_tiles), rows, :]
        vtw = jnp.concatenate([vt_tiles[n] for n in range(window_tiles)] + [cvt_scr[hp, rows, :]],
                              axis=1)
        lhs = jnp.concatenate([vtw, jnp.ones((16, s_scr.shape[1]), BF16)], axis=0)
        o_t = jnp.dot(lhs, p, preferred_element_type=F32)
        ot_scr[hp, gi, rows, :] = o_t[0:C_HD] / o_t[C_HD:C_HD + 1]

    n_slots = s_scr.shape[0]
    assert n_slots > SCORE_LEAD_LAT and n_units % n_slots == 0

    def pipeline_step(j, carry):
        for i in range(n_slots):
            score_stage(n_slots * j + i + SCORE_LEAD_LAT, (i + SCORE_LEAD_LAT) % n_slots)
            value_stage(n_slots * j + i, i)
        return carry

    gate = mod_ref[0, :, 2 * D_MODEL:3 * D_MODEL]

    def output_stage(gi):
        rows = slice(gi * group_tokens, (gi + 1) * group_tokens)
        attn = jnp.concatenate([ot_scr[t, gi].T for t in range(N_HEAD_PAIRS)], axis=1)
        yc = (attn * g_ref[rows, :].astype(F32)).astype(BF16)
        y = jnp.concatenate([yab_ref[rows, :], yc], axis=1)
        xo_ref[rows, :] = _post_norm(x_ref[rows, :], y, gate, w_out_ref, ln_g_ref, ln_b_ref)

    n_tail = 2 * groups_per_block
    assert (n_units - n_tail) % n_slots == 0
    for u in range(SCORE_LEAD_LAT):
        score_stage(jnp.int32(u), u)
    lax.fori_loop(0, (n_units - n_tail) // n_slots, pipeline_step, 0)
    for u in range(n_units - n_tail, n_units):
        if u + SCORE_LEAD_LAT < n_units:
            score_stage(jnp.int32(u + SCORE_LEAD_LAT), (u + SCORE_LEAD_LAT) % n_slots)
        value_stage(jnp.int32(u), u % n_slots)
        if u % 2 == 1:
            output_stage((u - (n_units - n_tail)) // 2)


def _lat_attn(layer, xs, yab, q, k, vt, gc, cache_k, cache_v, tbl, mod, lw, seq_len):
    n_tok = xs.shape[0]
    n_batch = n_tok // seq_len
    blocks_per_seq = seq_len // ATTN_TOKEN_BLOCK
    past = cache_k.shape[2]
    group_tokens = GROUP_ROWS * GRID_W
    cache_spec = pl.BlockSpec((None, None, past, C_W), lambda b, r: (b, layer, 0, 0),
                              pipeline_mode=pl.Buffered(1))
    tok_spec = lambda width: pl.BlockSpec((ATTN_TOKEN_BLOCK, width),
                                          lambda b, r: (b * blocks_per_seq + r, 0))
    return pl.pallas_call(
        _lat_attn_kernel,
        grid=(n_batch, blocks_per_seq),
        in_specs=[
            pl.BlockSpec(memory_space=pltpu.SMEM),
            tok_spec(D_MODEL),
            tok_spec(A_W + B_W),
            pl.BlockSpec((N_HEAD_PAIRS, ATTN_TOKEN_BLOCK // group_tokens, LANES, group_tokens),
                         lambda b, r: (0, b * blocks_per_seq + r, 0, 0)),
            tok_spec(C_W),
            pl.BlockSpec((N_HEAD_PAIRS, seq_len, LANES), lambda b, r: (0, b, 0)),
            pl.BlockSpec((N_HEAD_PAIRS, seq_len // LANES, LANES, LANES), lambda b, r: (0, b, 0, 0)),
            cache_spec,
            cache_spec,
            _layer_spec(tbl.shape[1:], layer),
            pl.BlockSpec((1, 1, 3 * D_MODEL), lambda b, r: (b, 0, 0)),
            _layer_spec((D_MODEL, D_MODEL), layer),
            _const_spec((1, D_MODEL)),
            _const_spec((1, D_MODEL)),
        ],
        out_specs=tok_spec(D_MODEL),
        out_shape=jax.ShapeDtypeStruct((n_tok, D_MODEL), F32),
        scratch_shapes=[
            pltpu.VMEM((4, WINDOW_ROWS * GRID_W + past, group_tokens), F32),
            pltpu.VMEM((4, 1, group_tokens), F32),
            pltpu.VMEM((N_HEAD_PAIRS, ATTN_TOKEN_BLOCK // group_tokens, LANES, group_tokens), F32),
            pltpu.VMEM((N_HEAD_PAIRS, past, LANES), BF16),
            pltpu.VMEM((N_HEAD_PAIRS, LANES, past), BF16),
        ],
        compiler_params=pltpu.CompilerParams(
            dimension_semantics=("arbitrary", "arbitrary"),
            vmem_limit_bytes=VMEM_LIMIT_BYTES),
        name=f"lat_attn_{layer}",
    )(jnp.asarray(_bias_entry_table()), xs, yab, q, gc, k, vt, cache_k, cache_v, tbl, mod,
      lw["w_out"], lw["ln_g"], lw["ln_b"])


def _bias_table_kernel(rp_ref, o_ref):
    n_taps = 2 * WIN_COLS - 1
    n_dr = 2 * WIN_ROWS - 1
    cp = lax.broadcasted_iota(jnp.int32, (GRID_W, LANES), 0)
    lane = lax.broadcasted_iota(jnp.int32, (GRID_W, LANES), 1)
    c = lane & (GRID_W - 1)
    cs = jnp.clip(c - WIN_COLS // 2, 0, GRID_W - WIN_COLS)
    valid = (cp >= cs) & (cp < cs + WIN_COLS)
    low = lane < GRID_W
    neg = jnp.full((GRID_W, LANES), NEG_INF, F32)

    def diagonal(head, e):
        row = jnp.broadcast_to(rp_ref[0, head, e:e + 1, :], (GRID_W, LANES))
        return pltpu.roll(row, LANES - n_taps // 2, 1, stride=1, stride_axis=0) * LOG2_E

    for head in range(C_HEADS):
        diag = [diagonal(head, e) for e in range(n_dr)]

        def pair(e_low, e_high):
            lo = neg if e_low is None else diag[e_low]
            hi = neg if e_high is None else diag[e_high]
            return jnp.where(valid, jnp.where(low, lo, hi), NEG_INF)

        for e in range(n_dr + 1):
            o_ref[0, head, e] = pair(e if e < n_dr else None, e - 1 if e >= 1 else None)
        o_ref[0, head, ENTRY_MASKED] = neg
        o_ref[0, head, ENTRY_LOW_ONLY] = pair(3, None)
        o_ref[0, head, ENTRY_HIGH_ONLY] = pair(None, 10)


def _bias_tables(rpb):
    n_dr, n_taps = rpb.shape[2], rpb.shape[3]
    rp = jnp.pad(rpb[..., ::-1], ((0, 0), (0, 0), (0, 16 - n_dr), (0, GRID_W - n_taps)),
                 constant_values=NEG_INF)
    rp = jnp.concatenate([rp, rp], axis=-1)
    tbl = pl.pallas_call(
        _bias_table_kernel,
        grid=(DEPTH,),
        in_specs=[pl.BlockSpec((1, C_HEADS, 16, LANES), lambda l: (l, 0, 0, 0))],
        out_specs=pl.BlockSpec((1, C_HEADS, N_BIAS_ENTRIES, GRID_W, LANES), lambda l: (l, 0, 0, 0, 0)),
        out_shape=jax.ShapeDtypeStruct((DEPTH, C_HEADS, N_BIAS_ENTRIES, GRID_W, LANES), F32),
        compiler_params=pltpu.CompilerParams(dimension_semantics=("arbitrary",)),
        name="bias_tables",
    )(rp)
    return tbl.reshape(DEPTH, C_HEADS * N_BIAS_ENTRIES, GRID_W, LANES)


def _layer_weights(l, w_in_bf, w_out_bf, conv_w, gmlp_ln_g, gmlp_ln_b, ws_pairs, bs_lanes, ln_g, ln_b):
    return {
        "w_in": w_in_bf, "w_out": w_out_bf, "conv_w": conv_w[l],
        "gln_g": gmlp_ln_g[l][None], "gln_b": gmlp_ln_b[l][None],
        "ws": ws_pairs[l], "bs": bs_lanes[l],
        "ln_g": ln_g[l][None], "ln_b": ln_b[l][None],
    }


def kernel(x_prompt, x_sample, cache_k, cache_v, c, c_ctx, w_ada, b_ada, w_in, conv_w, gmlp_ln_g,
           gmlp_ln_b, w_spatial, b_spatial, rpb, w_out, ln_g, ln_b):
    batch, seq, d = x_prompt.shape
    dec_batch, dec_seq, _ = x_sample.shape
    past = cache_k.shape[2]

    n_mod_rows = 8
    cvec = jnp.concatenate(
        [c_ctx[None], c, jnp.zeros((n_mod_rows - 1 - dec_batch, d), F32)], axis=0)
    mod = _modulation(cvec, w_ada, b_ada)

    w_in_bf = w_in.astype(BF16)
    w_out_bf = w_out.astype(BF16)
    ws_pairs = w_spatial.reshape(DEPTH, 2, 2, CHUNK, CHUNK).transpose(0, 1, 3, 2, 4)
    ws_pairs = ws_pairs.reshape(DEPTH, 2, CHUNK, 2 * CHUNK).astype(BF16)
    bs_lanes = jnp.repeat(b_spatial.reshape(DEPTH, 2, 2, CHUNK).transpose(0, 1, 3, 2), C_HD, axis=-1)

    cache_k2 = cache_k.reshape(dec_batch, DEPTH, past, C_W)
    cache_v2 = cache_v.reshape(dec_batch, DEPTH, past, C_W)
    tables = _bias_tables(rpb)

    xp = x_prompt.reshape(batch * seq, d)
    xs = x_sample.reshape(dec_batch * dec_seq, d)
    kv = None
    for l in range(DEPTH):
        lw = _layer_weights(l, w_in_bf, w_out_bf, conv_w, gmlp_ln_g, gmlp_ln_b, ws_pairs, bs_lanes,
                            ln_g, ln_b)
        xp, new_k, new_v = _ctx_layer(l, xp, mod[l, 0:1][None], lw, kv)
        kv = (new_k, new_v)
        mod_lat = mod[l, 1:1 + dec_batch][:, None]
        yab, q, k, vt, gc = _lat_proj(l, xs, mod_lat, lw, dec_seq)
        xs = _lat_attn(l, xs, yab, q, k, vt, gc, cache_k2, cache_v2, tables, mod_lat, lw, dec_seq)

    new_k, new_v = kv
    return (xp.reshape(batch, seq, d), xs.reshape(dec_batch, dec_seq, d),
            new_k.reshape(batch, DEPTH, seq, C_HEADS, C_HD),
            new_v.reshape(batch, DEPTH, seq, C_HEADS, C_HD))
```

```python
import functools

import jax
import jax.numpy as jnp
import numpy as np
from jax import lax
from jax.experimental import pallas as pl
from jax.experimental.pallas import tpu as pltpu

F32 = jnp.float32
BF16 = jnp.bfloat16

D_MODEL = 1024
DEPTH = 4
GRID_W = 64
A_W = 256
B_W = 256
CHUNK = 128
C_W = 512
C_HEADS = 8
C_HD = 64
WIN_ROWS = 8
WIN_COLS = 16
IN_W = 3840
ALPHA = (2 * DEPTH) ** 0.25
ATTN_SCALE = C_HD ** -0.5
NEG_INF = -1e30
LN_EPS = 1e-5
LOG2_E = 1.4426950408889634

LANES = 128
N_HEAD_PAIRS = C_W // LANES
A_COLS = (0, 4 * A_W)
B_COLS = (4 * A_W, 4 * A_W + 3 * B_W)
C_COLS = (4 * A_W + 3 * B_W, IN_W)

TOKEN_BLOCK = 512
ATTN_TOKEN_BLOCK = 1024
PROJ_TOKEN_BLOCK = 2048
CTX_TOKEN_BLOCK = 1024
SLAB_ROWS = 256
SCORE_LEAD_LAT = 2
SCORE_LEAD = 8
GROUP_ROWS = 4
WINDOW_ROWS = 12
ENTRY_MASKED = 16
ENTRY_LOW_ONLY = 17
ENTRY_HIGH_ONLY = 18
N_BIAS_ENTRIES = 19
VMEM_LIMIT_BYTES = 58 * 1024 * 1024


def _layernorm(x):
    mu = jnp.mean(x, axis=-1, keepdims=True)
    xc = x - mu
    var = jnp.mean(xc * xc, axis=-1, keepdims=True)
    return xc * lax.rsqrt(var + LN_EPS)


def _silu(x):
    return x / (1.0 + jnp.exp(-x))


def _lane_is_low(shape):
    return lax.broadcasted_iota(jnp.int32, shape, len(shape) - 1) < C_HD


def _mod_kernel(c_ref, w_ref, b_ref, o_ref):
    s = _silu(c_ref[...])
    w = w_ref[0]
    s_hi, w_hi = s.astype(BF16), w.astype(BF16)
    s_lo = (s - s_hi.astype(F32)).astype(BF16)
    w_lo = (w - w_hi.astype(F32)).astype(BF16)
    o_ref[0] = (jnp.dot(s_hi, w_hi, preferred_element_type=F32)
                + (jnp.dot(s_hi, w_lo, preferred_element_type=F32)
                   + jnp.dot(s_lo, w_hi, preferred_element_type=F32))) + b_ref[0]


def _modulation(cvec, w_ada, b_ada):
    n_rows = cvec.shape[0]
    col_block = 1024
    return pl.pallas_call(
        _mod_kernel,
        grid=(DEPTH, 3 * D_MODEL // col_block),
        in_specs=[
            pl.BlockSpec((n_rows, D_MODEL), lambda l, j: (0, 0)),
            pl.BlockSpec((1, D_MODEL, col_block), lambda l, j: (l, 0, j)),
            pl.BlockSpec((1, 1, col_block), lambda l, j: (l, 0, j)),
        ],
        out_specs=pl.BlockSpec((1, n_rows, col_block), lambda l, j: (l, 0, j)),
        out_shape=jax.ShapeDtypeStruct((DEPTH, n_rows, 3 * D_MODEL), F32),
        compiler_params=pltpu.CompilerParams(
            dimension_semantics=("arbitrary", "arbitrary"), vmem_limit_bytes=VMEM_LIMIT_BYTES),
        name="adaln_modulation",
    )(cvec, w_ada, b_ada.reshape(DEPTH, 1, 3 * D_MODEL))


def _modulated_norm(x, mod_ref):
    shift = mod_ref[0, :, 0:D_MODEL]
    scale = mod_ref[0, :, D_MODEL:2 * D_MODEL]
    return (_layernorm(x) * (1.0 + scale) + shift).astype(BF16)


def _modulated_norm_parts(x_ref, mod_ref):
    return [_modulated_norm(x_ref[lo:lo + SLAB_ROWS, :], mod_ref)
            for lo in range(0, x_ref.shape[0], SLAB_ROWS)]


def _project(h_parts, w_in_ref, cols):
    return jnp.concatenate(
        [jnp.dot(h, w_in_ref[:, cols[0]:cols[1]], preferred_element_type=F32) for h in h_parts],
        axis=0)


def _conv_halo(h_halo, w_in_ref):
    xh = jnp.dot(h_halo, w_in_ref[:, 0:A_W], preferred_element_type=F32)
    ch = jnp.dot(h_halo, w_in_ref[:, 2 * A_W:3 * A_W], preferred_element_type=F32)
    return ch * xh


def _conv_branch(pa, z_halo, pos, seq_len, conv_w_ref):
    tb = pa.shape[0]
    xa, ba = pa[:, 0:A_W], pa[:, A_W:2 * A_W]
    ca, ga = pa[:, 2 * A_W:3 * A_W], pa[:, 3 * A_W:4 * A_W]
    z = ca * xa
    z_prev = pltpu.roll(z, 1, 0)
    z_next = pltpu.roll(z, tb - 1, 0)
    if z_halo is not None:
        row = lax.broadcasted_iota(jnp.int32, (tb, 1), 0)
        z_prev = jnp.where(row == 0, z_halo[7:8, :], z_prev)
        z_next = jnp.where(row == tb - 1, z_halo[8:9, :], z_next)
    z_prev = jnp.where(pos == 0, 0.0, z_prev)
    z_next = jnp.where(pos == seq_len - 1, 0.0, z_next)
    cw = conv_w_ref[...]
    conv = cw[0:1, :] * z_prev + cw[1:2, :] * z + cw[2:3, :] * z_next
    return ba * conv * _silu(ga)


def _gmlp_branch(pb, gln_g_ref, gln_b_ref, ws_ref, bs_ref):
    tb = pb.shape[0]
    u, v, gb = pb[:, 0:B_W], pb[:, B_W:2 * B_W], pb[:, 2 * B_W:3 * B_W]
    vn = _layernorm(v) * gln_g_ref[...] + gln_b_ref[...]
    low = _lane_is_low((CHUNK, LANES))
    chunks = []
    for c in range(tb // CHUNK):
        tiles = []
        for t in range(B_W // LANES):
            vt = vn[c * CHUNK:(c + 1) * CHUNK, t * LANES:(t + 1) * LANES]
            rhs = jnp.concatenate([jnp.where(low, vt, 0.0), jnp.where(low, 0.0, vt)],
                                  axis=0).astype(BF16)
            tiles.append(jnp.dot(ws_ref[t], rhs, preferred_element_type=F32) + bs_ref[t])
        chunks.append(jnp.concatenate(tiles, axis=1))
    sv = jnp.concatenate(chunks, axis=0)
    return u * sv * _silu(gb)


def _post_norm(x, y_bf16, gate, w_out_ref, ln_g_ref, ln_b_ref):
    out = jnp.dot(y_bf16, w_out_ref[...], preferred_element_type=F32)
    return _layernorm(ALPHA * x + gate * out) * ln_g_ref[...] + ln_b_ref[...]


def _ctx_layer_kernel(seq_len, casts_next, x_ref, mod_ref, w_in_ref, conv_w_ref, gln_g_ref, gln_b_ref,
                      ws_ref, bs_ref, w_out_ref, ln_g_ref, ln_b_ref, *rest):
    if casts_next:
        rest[-1][...] = rest[0][...].astype(BF16)
        rest = rest[1:-1]
    xo_ref, ko_ref, vo_ref = rest[-3:]
    tb = x_ref.shape[0]
    n_seq = tb // seq_len
    h = _modulated_norm_parts(x_ref, mod_ref)
    pos = lax.broadcasted_iota(jnp.int32, (tb, 1), 0) & (seq_len - 1)
    pb = _project(h, w_in_ref, B_COLS)
    pa = _project(h, w_in_ref, A_COLS)
    yb = _gmlp_branch(pb, gln_g_ref, gln_b_ref, ws_ref, bs_ref)
    pc = _project(h, w_in_ref, C_COLS)
    ya = _conv_branch(pa, None, pos, seq_len, conv_w_ref)

    q, k = pc[:, 0:C_W], pc[:, C_W:2 * C_W]
    v, gc = pc[:, 2 * C_W:3 * C_W], pc[:, 3 * C_W:4 * C_W]
    qs = q * (ATTN_SCALE * LOG2_E)
    gate = mod_ref[0, :, 2 * D_MODEL:3 * D_MODEL]
    row_half = lax.shift_right_logical(lax.broadcasted_iota(jnp.int32, (LANES, seq_len), 0), 6)
    ones_rows = jnp.ones((16, seq_len), BF16)

    for s in range(n_seq):
        rows = slice(s * seq_len, (s + 1) * seq_len)
        ko_ref[s, 0] = k[rows, :]
        vo_ref[s, 0] = v[rows, :]
        for later in range(1, ko_ref.shape[1]):
            ko_ref[s, later] = jnp.zeros((seq_len, C_W), F32)
            vo_ref[s, later] = jnp.zeros((seq_len, C_W), F32)

    def scores(s, head):
        rows = slice(s * seq_len, (s + 1) * seq_len)
        lanes = slice((head // 2) * LANES, (head // 2 + 1) * LANES)
        k_t = k[rows, lanes].astype(BF16)
        qt = qs[rows, lanes].T.astype(BF16)
        qt_h = jnp.where(row_half == head % 2, qt, jnp.zeros_like(qt))
        return jnp.dot(k_t, qt_h, preferred_element_type=F32)

    def values(s, head, sc):
        rows = slice(s * seq_len, (s + 1) * seq_len)
        lanes = slice((head // 2) * LANES, (head // 2 + 1) * LANES)
        vt = v[rows, lanes].T.astype(BF16)
        p = jnp.exp2(sc - jnp.max(sc, axis=0, keepdims=True)).astype(BF16)
        lhs = jnp.concatenate([vt[(head % 2) * C_HD:(head % 2 + 1) * C_HD, :], ones_rows], axis=0)
        o_t = jnp.dot(lhs, p, preferred_element_type=F32)
        return o_t[0:C_HD] / o_t[C_HD:C_HD + 1]

    def output(s, head_out):
        rows = slice(s * seq_len, (s + 1) * seq_len)
        attn = jnp.concatenate(
            [jnp.concatenate(head_out[2 * t:2 * t + 2], axis=0).T for t in range(N_HEAD_PAIRS)],
            axis=1)
        yc = attn * _silu(gc[rows, :])
        y = jnp.concatenate([ya[rows, :], yb[rows, :], yc], axis=1).astype(BF16)
        xo_ref[rows, :] = _post_norm(x_ref[rows, :], y, gate, w_out_ref, ln_g_ref, ln_b_ref)

    units = [(s, head) for s in range(n_seq) for head in range(C_HEADS)]
    pending, head_out = {}, {s: [] for s in range(n_seq)}
    for step in range(len(units) + SCORE_LEAD):
        if step < len(units):
            pending[units[step]] = scores(*units[step])
        if step >= SCORE_LEAD:
            s, head = units[step - SCORE_LEAD]
            head_out[s].append(values(s, head, pending.pop((s, head))))
            if head == C_HEADS - 1:
                output(s, head_out[s])


def _const_spec(shape):
    return pl.BlockSpec(shape, lambda *_: (0,) * len(shape), pipeline_mode=pl.Buffered(1))


def _layer_spec(shape, layer):
    return pl.BlockSpec((None, *shape), lambda *_: (layer,) + (0,) * len(shape),
                        pipeline_mode=pl.Buffered(1))


def _weight_specs(layer):
    return [
        _const_spec((D_MODEL, IN_W)),
        _const_spec((3, A_W)),
        _const_spec((1, B_W)),
        _const_spec((1, B_W)),
        _const_spec((B_W // LANES, CHUNK, 2 * CHUNK)),
        _const_spec((B_W // LANES, CHUNK, LANES)),
    ]


def _ctx_layer(layer, xp, mod, lw, kv_prev, w_in_f32):
    n_tok = xp.shape[0]
    seq_len = 256
    n_seq_total = n_tok // seq_len
    kv_shape = jax.ShapeDtypeStruct((n_seq_total, DEPTH, seq_len, C_W), F32)
    block = TOKEN_BLOCK if kv_prev is None else CTX_TOKEN_BLOCK
    if kv_prev is None:
        kv_spec = pl.BlockSpec((block // seq_len, DEPTH, seq_len, C_W), lambda i: (i, 0, 0, 0))
    else:
        kv_spec = pl.BlockSpec((block // seq_len, 1, seq_len, C_W), lambda i: (i, layer, 0, 0))
    in_specs = [
        pl.BlockSpec((block, D_MODEL), lambda i: (i, 0)),
        _const_spec((1, 1, 3 * D_MODEL)),
        *_weight_specs(layer),
        _layer_spec((D_MODEL, D_MODEL), layer),
        _const_spec((1, D_MODEL)),
        _const_spec((1, D_MODEL)),
    ]
    args = [xp, mod, lw["w_in"], lw["conv_w"], lw["gln_g"], lw["gln_b"], lw["ws"], lw["bs"],
            lw["w_out"], lw["ln_g"], lw["ln_b"]]
    out_specs = [pl.BlockSpec((block, D_MODEL), lambda i: (i, 0)), kv_spec, kv_spec]
    out_shape = [jax.ShapeDtypeStruct((n_tok, D_MODEL), F32), kv_shape, kv_shape]
    casts_next = layer + 1 < DEPTH
    if casts_next:
        cast_rows = D_MODEL // (n_tok // block)
        in_specs.append(pl.BlockSpec((None, cast_rows, IN_W), lambda i: (layer + 1, i, 0)))
        args.append(w_in_f32)
        out_specs.append(pl.BlockSpec((cast_rows, IN_W), lambda i: (i, 0)))
        out_shape.append(jax.ShapeDtypeStruct((D_MODEL, IN_W), BF16))
    aliases = {}
    if kv_prev is not None:
        in_specs += [pl.BlockSpec(memory_space=pl.ANY), pl.BlockSpec(memory_space=pl.ANY)]
        aliases = {len(args): 1, len(args) + 1: 2}
        args += list(kv_prev)
    outs = pl.pallas_call(
        functools.partial(_ctx_layer_kernel, seq_len, casts_next),
        grid=(n_tok // block,),
        in_specs=in_specs,
        out_specs=out_specs,
        out_shape=out_shape,
        input_output_aliases=aliases,
        compiler_params=pltpu.CompilerParams(
            dimension_semantics=("arbitrary",), vmem_limit_bytes=VMEM_LIMIT_BYTES),
        name=f"ctx_layer_{layer}",
    )(*args)
    return tuple(outs) if casts_next else (*outs, None)


def _lat_proj_kernel(seq_len, x_ref, xprev_ref, xnext_ref, mod_ref, w_in_ref, conv_w_ref, gln_g_ref,
                     gln_b_ref, ws_ref, bs_ref, yab_ref, qt_ref, k_ref, vt_ref, g_ref):
    tb = x_ref.shape[0]
    group_tokens = GROUP_ROWS * GRID_W
    h = _modulated_norm_parts(x_ref, mod_ref)
    h_halo = _modulated_norm(jnp.concatenate([xprev_ref[0], xnext_ref[0]], axis=0), mod_ref)
    pos = (pl.program_id(0) * tb + lax.broadcasted_iota(jnp.int32, (tb, 1), 0)) & (seq_len - 1)
    z_halo = _conv_halo(h_halo, w_in_ref)
    pb = _project(h, w_in_ref, B_COLS)
    pa = _project(h, w_in_ref, A_COLS)
    yb = _gmlp_branch(pb, gln_g_ref, gln_b_ref, ws_ref, bs_ref)
    pvg = _project(h, w_in_ref, (C_COLS[0] + 2 * C_W, C_COLS[1]))
    ya = _conv_branch(pa, z_halo, pos, seq_len, conv_w_ref)
    yab_ref[...] = jnp.concatenate([ya, yb], axis=1).astype(BF16)
    pq = _project(h, w_in_ref, (C_COLS[0], C_COLS[0] + C_W))
    v = pvg[:, 0:C_W]
    for t in range(N_HEAD_PAIRS):
        for n in range(tb // LANES):
            vt_ref[t, n] = v[n * LANES:(n + 1) * LANES, t * LANES:(t + 1) * LANES].T.astype(BF16)
    g_ref[...] = _silu(pvg[:, C_W:2 * C_W]).astype(BF16)
    k = _project(h, w_in_ref, (C_COLS[0] + C_W, C_COLS[0] + 2 * C_W))
    qs = pq * (ATTN_SCALE * LOG2_E)
    for t in range(N_HEAD_PAIRS):
        for n in range(tb // group_tokens):
            qt_ref[t, n] = qs[n * group_tokens:(n + 1) * group_tokens,
                              t * LANES:(t + 1) * LANES].T.astype(BF16)
    for t in range(N_HEAD_PAIRS):
        k_ref[t] = k[:, t * LANES:(t + 1) * LANES].astype(BF16)


def _lat_proj(layer, xs, mod, lw, seq_len):
    n_tok = xs.shape[0]
    block = PROJ_TOKEN_BLOCK
    blocks_per_seq = seq_len // block
    n_halo = n_tok // 8
    halo_per_block = block // 8
    xs8 = xs.reshape(n_halo, 8, D_MODEL)
    group_tokens = GROUP_ROWS * GRID_W
    pair_shape = jax.ShapeDtypeStruct((N_HEAD_PAIRS, n_tok, LANES), BF16)
    pair_spec = pl.BlockSpec((N_HEAD_PAIRS, block, LANES), lambda i: (0, i, 0))
    return pl.pallas_call(
        functools.partial(_lat_proj_kernel, seq_len),
        grid=(n_tok // block,),
        in_specs=[
            pl.BlockSpec((block, D_MODEL), lambda i: (i, 0)),
            pl.BlockSpec((1, 8, D_MODEL), lambda i: (jnp.maximum(i * halo_per_block - 1, 0), 0, 0)),
            pl.BlockSpec((1, 8, D_MODEL),
                         lambda i: (jnp.minimum((i + 1) * halo_per_block, n_halo - 1), 0, 0)),
            pl.BlockSpec((1, 1, 3 * D_MODEL), lambda i: (i // blocks_per_seq, 0, 0)),
            *_weight_specs(layer),
        ],
        out_specs=[
            pl.BlockSpec((block, A_W + B_W), lambda i: (i, 0)),
            pl.BlockSpec((N_HEAD_PAIRS, block // group_tokens, LANES, group_tokens),
                         lambda i: (0, i, 0, 0)),
            pair_spec,
            pl.BlockSpec((N_HEAD_PAIRS, block // LANES, LANES, LANES), lambda i: (0, i, 0, 0)),
            pl.BlockSpec((block, C_W), lambda i: (i, 0)),
        ],
        out_shape=[
            jax.ShapeDtypeStruct((n_tok, A_W + B_W), BF16),
            jax.ShapeDtypeStruct((N_HEAD_PAIRS, n_tok // group_tokens, LANES, group_tokens), BF16),
            pair_shape,
            jax.ShapeDtypeStruct((N_HEAD_PAIRS, n_tok // LANES, LANES, LANES), BF16),
            jax.ShapeDtypeStruct((n_tok, C_W), BF16),
        ],
        compiler_params=pltpu.CompilerParams(
            dimension_semantics=("arbitrary",), vmem_limit_bytes=VMEM_LIMIT_BYTES),
        name=f"lat_proj_{layer}",
    )(xs, xs8, xs8, mod, lw["w_in"], lw["conv_w"], lw["gln_g"], lw["gln_b"], lw["ws"], lw["bs"])


GROUP_INTERIOR, GROUP_FIRST, GROUP_LAST = 0, 1, 2


def _bias_entry_table():
    table = np.zeros((3, WINDOW_ROWS * (GROUP_ROWS // 2)), np.int32)
    for kind, base in ((GROUP_INTERIOR, 3), (GROUP_FIRST, 7), (GROUP_LAST, -1)):
        for j in range(WINDOW_ROWS):
            for ip in range(GROUP_ROWS // 2):
                e = base + j - 2 * ip
                if kind == GROUP_INTERIOR:
                    entry = (ENTRY_LOW_ONLY if e == 3 else ENTRY_HIGH_ONLY if e == 11
                             else ENTRY_MASKED if (e < 3 or e > 11) else e)
                elif kind == GROUP_FIRST:
                    entry = ENTRY_MASKED if j >= WIN_ROWS else e
                else:
                    entry = ENTRY_MASKED if j < WINDOW_ROWS - WIN_ROWS else e
                assert 0 <= entry < N_BIAS_ENTRIES
                table[kind, j * (GROUP_ROWS // 2) + ip] = entry
    return table


def _lat_attn_kernel(ent_ref, x_ref, yab_ref, qt_ref, g_ref, k_ref, vt_ref, ck_ref, cvt_ref, tbl_ref,
                     mod_ref, w_out_ref, ln_g_ref, ln_b_ref, xo_ref, s_scr, m_scr, ot_scr):
    rb = pl.program_id(1)
    tb = x_ref.shape[0]
    group_tokens = GROUP_ROWS * GRID_W
    window_tokens = WINDOW_ROWS * GRID_W
    window_tiles = window_tokens // LANES
    n_groups = (k_ref.shape[1] // GRID_W) // GROUP_ROWS
    groups_per_block = tb // group_tokens
    n_units = N_HEAD_PAIRS * groups_per_block * 2
    group_bits = groups_per_block.bit_length() - 1
    assert groups_per_block == 1 << group_bits

    def coords(u):
        hp = lax.shift_right_logical(u, 1 + group_bits)
        gi = lax.shift_right_logical(u, 1) & (groups_per_block - 1)
        parity = u & 1
        g = rb * groups_per_block + gi
        first_tile = jnp.clip(2 * g - 2, 0, (GRID_W - WINDOW_ROWS) // 2)
        return hp, gi, parity, g, first_tile

    def score_stage(u, slot):
        hp, gi, parity, g, first_tile = coords(u)
        start = pl.multiple_of(first_tile * LANES, LANES)
        kw = k_ref[hp, pl.ds(start, window_tokens), :]
        qt = qt_ref[hp, gi]
        row_half = lax.shift_right_logical(
            lax.broadcasted_iota(jnp.int32, (LANES, group_tokens), 0), 6)
        qt_h = jnp.where(row_half == parity, qt, jnp.zeros_like(qt))
        key_chunk = 256
        s_loc = jnp.concatenate(
            [jnp.dot(kw[c:c + key_chunk], qt_h, preferred_element_type=F32)
             for c in range(0, window_tokens, key_chunk)], axis=0)
        s_ctx = jnp.dot(ck_ref[hp, 0], qt_h, preferred_element_type=F32)
        head_base = (2 * hp + parity) * N_BIAS_ENTRIES
        kind = jnp.where(g == 0, GROUP_FIRST, jnp.where(g == n_groups - 1, GROUP_LAST, GROUP_INTERIOR))
        bias_rows = []
        for j in range(WINDOW_ROWS):
            tiles = []
            for ip in range(GROUP_ROWS // 2):
                tiles.append(tbl_ref[head_base + ent_ref[kind, j * (GROUP_ROWS // 2) + ip]])
            bias_rows.append(jnp.concatenate(tiles, axis=1))
        s_loc = s_loc + jnp.concatenate(bias_rows, axis=0)
        s_scr[slot, 0:window_tokens, :] = s_loc
        s_scr[slot, window_tokens:, :] = s_ctx
        m_scr[slot] = jnp.maximum(jnp.max(s_loc, axis=0, keepdims=True),
                                  jnp.max(s_ctx, axis=0, keepdims=True))

    def value_stage(u, slot):
        hp, gi, parity, g, first_tile = coords(u)
        p = jnp.exp2(s_scr[slot] - m_scr[slot]).astype(BF16)
        rows = pl.ds(pl.multiple_of(parity * C_HD, C_HD), C_HD)
        vt_tiles = vt_ref[hp, pl.ds(first_tile, window_tiles), rows, :]
        vtw = jnp.concatenate([vt_tiles[n] for n in range(window_tiles)] + [cvt_ref[hp, 0, rows, :]],
                              axis=1)
        lhs = jnp.concatenate([vtw, jnp.ones((16, s_scr.shape[1]), BF16)], axis=0)
        o_t = jnp.dot(lhs, p, preferred_element_type=F32)
        ot_scr[hp, gi, rows, :] = o_t[0:C_HD] / o_t[C_HD:C_HD + 1]

    n_slots = s_scr.shape[0]
    assert n_slots > SCORE_LEAD_LAT and n_units % n_slots == 0

    def pipeline_step(j, carry):
        for i in range(n_slots):
            score_stage(n_slots * j + i + SCORE_LEAD_LAT, (i + SCORE_LEAD_LAT) % n_slots)
            value_stage(n_slots * j + i, i)
        return carry

    gate = mod_ref[0, :, 2 * D_MODEL:3 * D_MODEL]

    def output_stage(gi):
        rows = slice(gi * group_tokens, (gi + 1) * group_tokens)
        attn = jnp.concatenate([ot_scr[t, gi].T for t in range(N_HEAD_PAIRS)], axis=1)
        yc = (attn * g_ref[rows, :].astype(F32)).astype(BF16)
        y = jnp.concatenate([yab_ref[rows, :], yc], axis=1)
        xo_ref[rows, :] = _post_norm(x_ref[rows, :], y, gate, w_out_ref, ln_g_ref, ln_b_ref)

    n_tail = 2 * groups_per_block
    assert (n_units - n_tail) % n_slots == 0
    for u in range(SCORE_LEAD_LAT):
        score_stage(jnp.int32(u), u)
    lax.fori_loop(0, (n_units - n_tail) // n_slots, pipeline_step, 0)
    for u in range(n_units - n_tail, n_units):
        if u + SCORE_LEAD_LAT < n_units:
            score_stage(jnp.int32(u + SCORE_LEAD_LAT), (u + SCORE_LEAD_LAT) % n_slots)
        value_stage(jnp.int32(u), u % n_slots)
        if u % 2 == 1:
            output_stage((u - (n_units - n_tail)) // 2)


def _lat_attn(layer, xs, yab, q, k, vt, gc, ck, cvt, tbl, mod, lw, seq_len):
    n_tok = xs.shape[0]
    n_batch = n_tok // seq_len
    blocks_per_seq = seq_len // ATTN_TOKEN_BLOCK
    past = ck.shape[2]
    group_tokens = GROUP_ROWS * GRID_W
    tok_spec = lambda width: pl.BlockSpec((ATTN_TOKEN_BLOCK, width),
                                          lambda b, r: (b * blocks_per_seq + r, 0))
    return pl.pallas_call(
        _lat_attn_kernel,
        grid=(n_batch, blocks_per_seq),
        in_specs=[
            pl.BlockSpec(memory_space=pltpu.SMEM),
            tok_spec(D_MODEL),
            tok_spec(A_W + B_W),
            pl.BlockSpec((N_HEAD_PAIRS, ATTN_TOKEN_BLOCK // group_tokens, LANES, group_tokens),
                         lambda b, r: (0, b * blocks_per_seq + r, 0, 0)),
            tok_spec(C_W),
            pl.BlockSpec((N_HEAD_PAIRS, seq_len, LANES), lambda b, r: (0, b, 0)),
            pl.BlockSpec((N_HEAD_PAIRS, seq_len // LANES, LANES, LANES), lambda b, r: (0, b, 0, 0)),
            pl.BlockSpec((N_HEAD_PAIRS, 1, past, LANES), lambda b, r: (0, b, 0, 0)),
            pl.BlockSpec((N_HEAD_PAIRS, 1, LANES, past), lambda b, r: (0, b, 0, 0)),
            _layer_spec(tbl.shape[1:], layer),
            pl.BlockSpec((1, 1, 3 * D_MODEL), lambda b, r: (b, 0, 0)),
            _layer_spec((D_MODEL, D_MODEL), layer),
            _const_spec((1, D_MODEL)),
            _const_spec((1, D_MODEL)),
        ],
        out_specs=tok_spec(D_MODEL),
        out_shape=jax.ShapeDtypeStruct((n_tok, D_MODEL), F32),
        scratch_shapes=[
            pltpu.VMEM((4, WINDOW_ROWS * GRID_W + past, group_tokens), F32),
            pltpu.VMEM((4, 1, group_tokens), F32),
            pltpu.VMEM((N_HEAD_PAIRS, ATTN_TOKEN_BLOCK // group_tokens, LANES, group_tokens), F32),
        ],
        compiler_params=pltpu.CompilerParams(
            dimension_semantics=("arbitrary", "arbitrary"),
            vmem_limit_bytes=VMEM_LIMIT_BYTES),
        name=f"lat_attn_{layer}",
    )(jnp.asarray(_bias_entry_table()), xs, yab, q, gc, k, vt, ck, cvt, tbl, mod, lw["w_out"],
      lw["ln_g"], lw["ln_b"])


def _bias_table_kernel(rp_ref, o_ref):
    n_taps = 2 * WIN_COLS - 1
    n_dr = 2 * WIN_ROWS - 1
    cp = lax.broadcasted_iota(jnp.int32, (GRID_W, LANES), 0)
    lane = lax.broadcasted_iota(jnp.int32, (GRID_W, LANES), 1)
    c = lane & (GRID_W - 1)
    cs = jnp.clip(c - WIN_COLS // 2, 0, GRID_W - WIN_COLS)
    valid = (cp >= cs) & (cp < cs + WIN_COLS)
    low = lane < GRID_W
    neg = jnp.full((GRID_W, LANES), NEG_INF, F32)

    def diagonal(head, e):
        row = jnp.broadcast_to(rp_ref[0, head, e:e + 1, :], (GRID_W, LANES))
        return pltpu.roll(row, LANES - n_taps // 2, 1, stride=1, stride_axis=0) * LOG2_E

    for head in range(C_HEADS):
        diag = [diagonal(head, e) for e in range(n_dr)]

        def pair(e_low, e_high):
            lo = neg if e_low is None else diag[e_low]
            hi = neg if e_high is None else diag[e_high]
            return jnp.where(valid, jnp.where(low, lo, hi), NEG_INF)

        for e in range(n_dr + 1):
            o_ref[0, head, e] = pair(e if e < n_dr else None, e - 1 if e >= 1 else None)
        o_ref[0, head, ENTRY_MASKED] = neg
        o_ref[0, head, ENTRY_LOW_ONLY] = pair(3, None)
        o_ref[0, head, ENTRY_HIGH_ONLY] = pair(None, 10)


def _bias_tables(rpb):
    n_dr, n_taps = rpb.shape[2], rpb.shape[3]
    rp = jnp.pad(rpb[..., ::-1], ((0, 0), (0, 0), (0, 16 - n_dr), (0, GRID_W - n_taps)),
                 constant_values=NEG_INF)
    rp = jnp.concatenate([rp, rp], axis=-1)
    tbl = pl.pallas_call(
        _bias_table_kernel,
        grid=(DEPTH,),
        in_specs=[pl.BlockSpec((1, C_HEADS, 16, LANES), lambda l: (l, 0, 0, 0))],
        out_specs=pl.BlockSpec((1, C_HEADS, N_BIAS_ENTRIES, GRID_W, LANES), lambda l: (l, 0, 0, 0, 0)),
        out_shape=jax.ShapeDtypeStruct((DEPTH, C_HEADS, N_BIAS_ENTRIES, GRID_W, LANES), F32),
        compiler_params=pltpu.CompilerParams(dimension_semantics=("arbitrary",)),
        name="bias_tables",
    )(rp)
    return tbl.reshape(DEPTH, C_HEADS * N_BIAS_ENTRIES, GRID_W, LANES)


def _layer_weights(l, w_in_bf, w_out_bf, conv_w, gmlp_ln_g, gmlp_ln_b, ws_pairs, bs_lanes, ln_g, ln_b):
    return {
        "w_in": w_in_bf, "w_out": w_out_bf, "conv_w": conv_w[l],
        "gln_g": gmlp_ln_g[l][None], "gln_b": gmlp_ln_b[l][None],
        "ws": ws_pairs[l], "bs": bs_lanes[l],
        "ln_g": ln_g[l][None], "ln_b": ln_b[l][None],
    }


def kernel(x_prompt, x_sample, cache_k, cache_v, c, c_ctx, w_ada, b_ada, w_in, conv_w, gmlp_ln_g,
           gmlp_ln_b, w_spatial, b_spatial, rpb, w_out, ln_g, ln_b):
    batch, seq, d = x_prompt.shape
    dec_batch, dec_seq, _ = x_sample.shape
    past = cache_k.shape[2]

    n_mod_rows = 8
    cvec = jnp.concatenate(
        [c_ctx[None], c, jnp.zeros((n_mod_rows - 1 - dec_batch, d), F32)], axis=0)
    mod = _modulation(cvec, w_ada, b_ada)

    w_in_bf = w_in[0].astype(BF16)
    w_out_bf = w_out.astype(BF16)
    ws_pairs = w_spatial.reshape(DEPTH, 2, 2, CHUNK, CHUNK).transpose(0, 1, 3, 2, 4)
    ws_pairs = ws_pairs.reshape(DEPTH, 2, CHUNK, 2 * CHUNK).astype(BF16)
    bs_lanes = jnp.repeat(b_spatial.reshape(DEPTH, 2, 2, CHUNK).transpose(0, 1, 3, 2), C_HD, axis=-1)

    def pair_major(cache):
        cache = cache.reshape(dec_batch, DEPTH, past, N_HEAD_PAIRS, LANES)
        return cache.transpose(1, 3, 0, 2, 4).astype(BF16)

    ck_all = pair_major(cache_k)
    cvt_all = jnp.swapaxes(pair_major(cache_v), -1, -2)
    tables = _bias_tables(rpb)

    xp = x_prompt.reshape(batch * seq, d)
    xs = x_sample.reshape(dec_batch * dec_seq, d)
    kv = None
    for l in range(DEPTH):
        lw = _layer_weights(l, w_in_bf, w_out_bf, conv_w, gmlp_ln_g, gmlp_ln_b, ws_pairs, bs_lanes,
                            ln_g, ln_b)
        xp, new_k, new_v, w_in_next = _ctx_layer(l, xp, mod[l, 0:1][None], lw, kv, w_in)
        kv = (new_k, new_v)
        mod_lat = mod[l, 1:1 + dec_batch][:, None]
        yab, q, k, vt, gc = _lat_proj(l, xs, mod_lat, lw, dec_seq)
        xs = _lat_attn(l, xs, yab, q, k, vt, gc, ck_all[l], cvt_all[l], tables, mod_lat, lw, dec_seq)
        w_in_bf = w_in_next

    new_k, new_v = kv
    return (xp.reshape(batch, seq, d), xs.reshape(dec_batch, dec_seq, d),
            new_k.reshape(batch, DEPTH, seq, C_HEADS, C_HD),
            new_v.reshape(batch, DEPTH, seq, C_HEADS, C_HD))
```

```python
import functools

import jax
import jax.numpy as jnp
import numpy as np
from jax import lax
from jax.experimental import pallas as pl
from jax.experimental.pallas import tpu as pltpu

F32 = jnp.float32
BF16 = jnp.bfloat16

D_MODEL = 1024
DEPTH = 4
GRID_W = 64
A_W = 256
B_W = 256
CHUNK = 128
C_W = 512
C_HEADS = 8
C_HD = 64
WIN_ROWS = 8
WIN_COLS = 16
IN_W = 3840
ALPHA = (2 * DEPTH) ** 0.25
ATTN_SCALE = C_HD ** -0.5
NEG_INF = -1e30
LN_EPS = 1e-5
LOG2_E = 1.4426950408889634

LANES = 128
N_HEAD_PAIRS = C_W // LANES
A_COLS = (0, 4 * A_W)
B_COLS = (4 * A_W, 4 * A_W + 3 * B_W)
C_COLS = (4 * A_W + 3 * B_W, IN_W)

TOKEN_BLOCK = 512
ATTN_TOKEN_BLOCK = 1024
PROJ_TOKEN_BLOCK = 2048
CTX_TOKEN_BLOCK = 1024
SLAB_ROWS = 256
SCORE_LEAD_LAT = 2
SCORE_LEAD = 8
GROUP_ROWS = 4
WINDOW_ROWS = 12
ENTRY_MASKED = 16
ENTRY_LOW_ONLY = 17
ENTRY_HIGH_ONLY = 18
N_BIAS_ENTRIES = 19
VMEM_LIMIT_BYTES = 58 * 1024 * 1024


def _layernorm(x):
    mu = jnp.mean(x, axis=-1, keepdims=True)
    xc = x - mu
    var = jnp.mean(xc * xc, axis=-1, keepdims=True)
    return xc * lax.rsqrt(var + LN_EPS)


def _silu(x):
    return x / (1.0 + jnp.exp(-x))


def _lane_is_low(shape):
    return lax.broadcasted_iota(jnp.int32, shape, len(shape) - 1) < C_HD


def _mod_kernel(c_ref, w_ref, b_ref, o_ref):
    s = _silu(c_ref[...])
    w = w_ref[0]
    s_hi, w_hi = s.astype(BF16), w.astype(BF16)
    s_lo = (s - s_hi.astype(F32)).astype(BF16)
    w_lo = (w - w_hi.astype(F32)).astype(BF16)
    o_ref[0] = (jnp.dot(s_hi, w_hi, preferred_element_type=F32)
                + (jnp.dot(s_hi, w_lo, preferred_element_type=F32)
                   + jnp.dot(s_lo, w_hi, preferred_element_type=F32))) + b_ref[0]


def _modulation(cvec, w_ada, b_ada):
    n_rows = cvec.shape[0]
    col_block = 1024
    return pl.pallas_call(
        _mod_kernel,
        grid=(DEPTH, 3 * D_MODEL // col_block),
        in_specs=[
            pl.BlockSpec((n_rows, D_MODEL), lambda l, j: (0, 0)),
            pl.BlockSpec((1, D_MODEL, col_block), lambda l, j: (l, 0, j)),
            pl.BlockSpec((1, 1, col_block), lambda l, j: (l, 0, j)),
        ],
        out_specs=pl.BlockSpec((1, n_rows, col_block), lambda l, j: (l, 0, j)),
        out_shape=jax.ShapeDtypeStruct((DEPTH, n_rows, 3 * D_MODEL), F32),
        compiler_params=pltpu.CompilerParams(
            dimension_semantics=("arbitrary", "arbitrary"), vmem_limit_bytes=VMEM_LIMIT_BYTES),
        name="adaln_modulation",
    )(cvec, w_ada, b_ada.reshape(DEPTH, 1, 3 * D_MODEL))


def _modulated_norm(x, mod_ref):
    shift = mod_ref[0, :, 0:D_MODEL]
    scale = mod_ref[0, :, D_MODEL:2 * D_MODEL]
    return (_layernorm(x) * (1.0 + scale) + shift).astype(BF16)


def _modulated_norm_parts(x_ref, mod_ref):
    return [_modulated_norm(x_ref[lo:lo + SLAB_ROWS, :], mod_ref)
            for lo in range(0, x_ref.shape[0], SLAB_ROWS)]


def _project(h_parts, w_in_ref, cols):
    return jnp.concatenate(
        [jnp.dot(h, w_in_ref[:, cols[0]:cols[1]], preferred_element_type=F32) for h in h_parts],
        axis=0)


def _conv_halo(h_halo, w_in_ref):
    xh = jnp.dot(h_halo, w_in_ref[:, 0:A_W], preferred_element_type=F32)
    ch = jnp.dot(h_halo, w_in_ref[:, 2 * A_W:3 * A_W], preferred_element_type=F32)
    return ch * xh


def _conv_branch(pa, z_halo, pos, seq_len, conv_w_ref):
    tb = pa.shape[0]
    xa, ba = pa[:, 0:A_W], pa[:, A_W:2 * A_W]
    ca, ga = pa[:, 2 * A_W:3 * A_W], pa[:, 3 * A_W:4 * A_W]
    z = ca * xa
    z_prev = pltpu.roll(z, 1, 0)
    z_next = pltpu.roll(z, tb - 1, 0)
    if z_halo is not None:
        row = lax.broadcasted_iota(jnp.int32, (tb, 1), 0)
        z_prev = jnp.where(row == 0, z_halo[7:8, :], z_prev)
        z_next = jnp.where(row == tb - 1, z_halo[8:9, :], z_next)
    z_prev = jnp.where(pos == 0, 0.0, z_prev)
    z_next = jnp.where(pos == seq_len - 1, 0.0, z_next)
    cw = conv_w_ref[...]
    conv = cw[0:1, :] * z_prev + cw[1:2, :] * z + cw[2:3, :] * z_next
    return ba * conv * _silu(ga)


def _gmlp_branch(pb, gln_g_ref, gln_b_ref, ws_ref, bs_ref):
    tb = pb.shape[0]
    u, v, gb = pb[:, 0:B_W], pb[:, B_W:2 * B_W], pb[:, 2 * B_W:3 * B_W]
    vn = _layernorm(v) * gln_g_ref[...] + gln_b_ref[...]
    low = _lane_is_low((CHUNK, LANES))
    chunks = []
    for c in range(tb // CHUNK):
        tiles = []
        for t in range(B_W // LANES):
            vt = vn[c * CHUNK:(c + 1) * CHUNK, t * LANES:(t + 1) * LANES]
            rhs = jnp.concatenate([jnp.where(low, vt, 0.0), jnp.where(low, 0.0, vt)],
                                  axis=0).astype(BF16)
            tiles.append(jnp.dot(ws_ref[t], rhs, preferred_element_type=F32) + bs_ref[t])
        chunks.append(jnp.concatenate(tiles, axis=1))
    sv = jnp.concatenate(chunks, axis=0)
    return u * sv * _silu(gb)


def _post_norm(x, y_bf16, gate, w_out_ref, ln_g_ref, ln_b_ref):
    out = jnp.dot(y_bf16, w_out_ref[...], preferred_element_type=F32)
    return _layernorm(ALPHA * x + gate * out) * ln_g_ref[...] + ln_b_ref[...]


def _ctx_layer_kernel(seq_len, casts_next, x_ref, mod_ref, w_in_ref, conv_w_ref, gln_g_ref, gln_b_ref,
                      ws_ref, bs_ref, w_out_ref, ln_g_ref, ln_b_ref, *rest):
    if casts_next:
        rest[-2][...] = rest[0][...].astype(BF16)
        rest[-1][...] = rest[1][...].astype(BF16)
        rest = rest[2:-2]
    xo_ref, ko_ref, vo_ref = rest[-3:]
    tb = x_ref.shape[0]
    n_seq = tb // seq_len
    h = _modulated_norm_parts(x_ref, mod_ref)
    pos = lax.broadcasted_iota(jnp.int32, (tb, 1), 0) & (seq_len - 1)
    pb = _project(h, w_in_ref, B_COLS)
    pa = _project(h, w_in_ref, A_COLS)
    yb = _gmlp_branch(pb, gln_g_ref, gln_b_ref, ws_ref, bs_ref)
    pc = _project(h, w_in_ref, C_COLS)
    ya = _conv_branch(pa, None, pos, seq_len, conv_w_ref)

    q, k = pc[:, 0:C_W], pc[:, C_W:2 * C_W]
    v, gc = pc[:, 2 * C_W:3 * C_W], pc[:, 3 * C_W:4 * C_W]
    qs = q * (ATTN_SCALE * LOG2_E)
    gate = mod_ref[0, :, 2 * D_MODEL:3 * D_MODEL]
    row_half = lax.shift_right_logical(lax.broadcasted_iota(jnp.int32, (LANES, seq_len), 0), 6)
    ones_rows = jnp.ones((16, seq_len), BF16)

    for s in range(n_seq):
        rows = slice(s * seq_len, (s + 1) * seq_len)
        ko_ref[s, 0] = k[rows, :]
        vo_ref[s, 0] = v[rows, :]
        for later in range(1, ko_ref.shape[1]):
            ko_ref[s, later] = jnp.zeros((seq_len, C_W), F32)
            vo_ref[s, later] = jnp.zeros((seq_len, C_W), F32)

    def scores(s, head):
        rows = slice(s * seq_len, (s + 1) * seq_len)
        lanes = slice((head // 2) * LANES, (head // 2 + 1) * LANES)
        k_t = k[rows, lanes].astype(BF16)
        qt = qs[rows, lanes].T.astype(BF16)
        qt_h = jnp.where(row_half == head % 2, qt, jnp.zeros_like(qt))
        return jnp.dot(k_t, qt_h, preferred_element_type=F32)

    def values(s, head, sc):
        rows = slice(s * seq_len, (s + 1) * seq_len)
        lanes = slice((head // 2) * LANES, (head // 2 + 1) * LANES)
        vt = v[rows, lanes].T.astype(BF16)
        p = jnp.exp2(sc - jnp.max(sc, axis=0, keepdims=True)).astype(BF16)
        lhs = jnp.concatenate([vt[(head % 2) * C_HD:(head % 2 + 1) * C_HD, :], ones_rows], axis=0)
        o_t = jnp.dot(lhs, p, preferred_element_type=F32)
        return o_t[0:C_HD] / o_t[C_HD:C_HD + 1]

    def output(s, head_out):
        rows = slice(s * seq_len, (s + 1) * seq_len)
        attn = jnp.concatenate(
            [jnp.concatenate(head_out[2 * t:2 * t + 2], axis=0).T for t in range(N_HEAD_PAIRS)],
            axis=1)
        yc = attn * _silu(gc[rows, :])
        y = jnp.concatenate([ya[rows, :], yb[rows, :], yc], axis=1).astype(BF16)
        xo_ref[rows, :] = _post_norm(x_ref[rows, :], y, gate, w_out_ref, ln_g_ref, ln_b_ref)

    units = [(s, head) for s in range(n_seq) for head in range(C_HEADS)]
    pending, head_out = {}, {s: [] for s in range(n_seq)}
    for step in range(len(units) + SCORE_LEAD):
        if step < len(units):
            pending[units[step]] = scores(*units[step])
        if step >= SCORE_LEAD:
            s, head = units[step - SCORE_LEAD]
            head_out[s].append(values(s, head, pending.pop((s, head))))
            if head == C_HEADS - 1:
                output(s, head_out[s])


def _const_spec(shape):
    return pl.BlockSpec(shape, lambda *_: (0,) * len(shape), pipeline_mode=pl.Buffered(1))


def _layer_spec(shape, layer):
    return pl.BlockSpec((None, *shape), lambda *_: (layer,) + (0,) * len(shape),
                        pipeline_mode=pl.Buffered(1))


def _weight_specs(layer):
    return [
        _const_spec((D_MODEL, IN_W)),
        _const_spec((3, A_W)),
        _const_spec((1, B_W)),
        _const_spec((1, B_W)),
        _const_spec((B_W // LANES, CHUNK, 2 * CHUNK)),
        _const_spec((B_W // LANES, CHUNK, LANES)),
    ]


def _ctx_layer(layer, xp, mod, lw, kv_prev, w_in_f32, w_out_f32):
    n_tok = xp.shape[0]
    seq_len = 256
    n_seq_total = n_tok // seq_len
    kv_shape = jax.ShapeDtypeStruct((n_seq_total, DEPTH, seq_len, C_W), F32)
    block = TOKEN_BLOCK if kv_prev is None else CTX_TOKEN_BLOCK
    if kv_prev is None:
        kv_spec = pl.BlockSpec((block // seq_len, DEPTH, seq_len, C_W), lambda i: (i, 0, 0, 0))
    else:
        kv_spec = pl.BlockSpec((block // seq_len, 1, seq_len, C_W), lambda i: (i, layer, 0, 0))
    in_specs = [
        pl.BlockSpec((block, D_MODEL), lambda i: (i, 0)),
        _const_spec((1, 1, 3 * D_MODEL)),
        *_weight_specs(layer),
        _const_spec((D_MODEL, D_MODEL)),
        _const_spec((1, D_MODEL)),
        _const_spec((1, D_MODEL)),
    ]
    args = [xp, mod, lw["w_in"], lw["conv_w"], lw["gln_g"], lw["gln_b"], lw["ws"], lw["bs"],
            lw["w_out"], lw["ln_g"], lw["ln_b"]]
    out_specs = [pl.BlockSpec((block, D_MODEL), lambda i: (i, 0)), kv_spec, kv_spec]
    out_shape = [jax.ShapeDtypeStruct((n_tok, D_MODEL), F32), kv_shape, kv_shape]
    casts_next = layer + 1 < DEPTH
    if casts_next:
        cast_rows = D_MODEL // (n_tok // block)
        for w_f32, width in ((w_in_f32, IN_W), (w_out_f32, D_MODEL)):
            in_specs.append(pl.BlockSpec((None, cast_rows, width), lambda i: (layer + 1, i, 0)))
            args.append(w_f32)
            out_specs.append(pl.BlockSpec((cast_rows, width), lambda i: (i, 0)))
            out_shape.append(jax.ShapeDtypeStruct((D_MODEL, width), BF16))
    aliases = {}
    if kv_prev is not None:
        in_specs += [pl.BlockSpec(memory_space=pl.ANY), pl.BlockSpec(memory_space=pl.ANY)]
        aliases = {len(args): 1, len(args) + 1: 2}
        args += list(kv_prev)
    outs = pl.pallas_call(
        functools.partial(_ctx_layer_kernel, seq_len, casts_next),
        grid=(n_tok // block,),
        in_specs=in_specs,
        out_specs=out_specs,
        out_shape=out_shape,
        input_output_aliases=aliases,
        compiler_params=pltpu.CompilerParams(
            dimension_semantics=("arbitrary",), vmem_limit_bytes=VMEM_LIMIT_BYTES),
        name=f"ctx_layer_{layer}",
    )(*args)
    return tuple(outs) if casts_next else (*outs, None, None)


def _lat_proj_kernel(seq_len, x_ref, xprev_ref, xnext_ref, mod_ref, w_in_ref, conv_w_ref, gln_g_ref,
                     gln_b_ref, ws_ref, bs_ref, yab_ref, qt_ref, k_ref, vt_ref, g_ref):
    tb = x_ref.shape[0]
    group_tokens = GROUP_ROWS * GRID_W
    h = _modulated_norm_parts(x_ref, mod_ref)
    h_halo = _modulated_norm(jnp.concatenate([xprev_ref[0], xnext_ref[0]], axis=0), mod_ref)
    pos = (pl.program_id(0) * tb + lax.broadcasted_iota(jnp.int32, (tb, 1), 0)) & (seq_len - 1)
    z_halo = _conv_halo(h_halo, w_in_ref)
    pb = _project(h, w_in_ref, B_COLS)
    pa = _project(h, w_in_ref, A_COLS)
    yb = _gmlp_branch(pb, gln_g_ref, gln_b_ref, ws_ref, bs_ref)
    pvg = _project(h, w_in_ref, (C_COLS[0] + 2 * C_W, C_COLS[1]))
    ya = _conv_branch(pa, z_halo, pos, seq_len, conv_w_ref)
    yab_ref[...] = jnp.concatenate([ya, yb], axis=1).astype(BF16)
    pq = _project(h, w_in_ref, (C_COLS[0], C_COLS[0] + C_W))
    v = pvg[:, 0:C_W]
    for t in range(N_HEAD_PAIRS):
        for n in range(tb // LANES):
            vt_ref[t, n] = v[n * LANES:(n + 1) * LANES, t * LANES:(t + 1) * LANES].T.astype(BF16)
    g_ref[...] = _silu(pvg[:, C_W:2 * C_W]).astype(BF16)
    k = _project(h, w_in_ref, (C_COLS[0] + C_W, C_COLS[0] + 2 * C_W))
    qs = pq * (ATTN_SCALE * LOG2_E)
    for t in range(N_HEAD_PAIRS):
        for n in range(tb // group_tokens):
            qt_ref[t, n] = qs[n * group_tokens:(n + 1) * group_tokens,
                              t * LANES:(t + 1) * LANES].T.astype(BF16)
    for t in range(N_HEAD_PAIRS):
        k_ref[t] = k[:, t * LANES:(t + 1) * LANES].astype(BF16)


def _lat_proj(layer, xs, mod, lw, seq_len):
    n_tok = xs.shape[0]
    block = PROJ_TOKEN_BLOCK
    blocks_per_seq = seq_len // block
    n_halo = n_tok // 8
    halo_per_block = block // 8
    xs8 = xs.reshape(n_halo, 8, D_MODEL)
    group_tokens = GROUP_ROWS * GRID_W
    pair_shape = jax.ShapeDtypeStruct((N_HEAD_PAIRS, n_tok, LANES), BF16)
    pair_spec = pl.BlockSpec((N_HEAD_PAIRS, block, LANES), lambda i: (0, i, 0))
    return pl.pallas_call(
        functools.partial(_lat_proj_kernel, seq_len),
        grid=(n_tok // block,),
        in_specs=[
            pl.BlockSpec((block, D_MODEL), lambda i: (i, 0)),
            pl.BlockSpec((1, 8, D_MODEL), lambda i: (jnp.maximum(i * halo_per_block - 1, 0), 0, 0)),
            pl.BlockSpec((1, 8, D_MODEL),
                         lambda i: (jnp.minimum((i + 1) * halo_per_block, n_halo - 1), 0, 0)),
            pl.BlockSpec((1, 1, 3 * D_MODEL), lambda i: (i // blocks_per_seq, 0, 0)),
            *_weight_specs(layer),
        ],
        out_specs=[
            pl.BlockSpec((block, A_W + B_W), lambda i: (i, 0)),
            pl.BlockSpec((N_HEAD_PAIRS, block // group_tokens, LANES, group_tokens),
                         lambda i: (0, i, 0, 0)),
            pair_spec,
            pl.BlockSpec((N_HEAD_PAIRS, block // LANES, LANES, LANES), lambda i: (0, i, 0, 0)),
            pl.BlockSpec((block, C_W), lambda i: (i, 0)),
        ],
        out_shape=[
            jax.ShapeDtypeStruct((n_tok, A_W + B_W), BF16),
            jax.ShapeDtypeStruct((N_HEAD_PAIRS, n_tok // group_tokens, LANES, group_tokens), BF16),
            pair_shape,
            jax.ShapeDtypeStruct((N_HEAD_PAIRS, n_tok // LANES, LANES, LANES), BF16),
            jax.ShapeDtypeStruct((n_tok, C_W), BF16),
        ],
        compiler_params=pltpu.CompilerParams(
            dimension_semantics=("arbitrary",), vmem_limit_bytes=VMEM_LIMIT_BYTES),
        name=f"lat_proj_{layer}",
    )(xs, xs8, xs8, mod, lw["w_in"], lw["conv_w"], lw["gln_g"], lw["gln_b"], lw["ws"], lw["bs"])


GROUP_INTERIOR, GROUP_FIRST, GROUP_LAST = 0, 1, 2


def _bias_entry_table():
    table = np.zeros((3, WINDOW_ROWS * (GROUP_ROWS // 2)), np.int32)
    for kind, base in ((GROUP_INTERIOR, 3), (GROUP_FIRST, 7), (GROUP_LAST, -1)):
        for j in range(WINDOW_ROWS):
            for ip in range(GROUP_ROWS // 2):
                e = base + j - 2 * ip
                if kind == GROUP_INTERIOR:
                    entry = (ENTRY_LOW_ONLY if e == 3 else ENTRY_HIGH_ONLY if e == 11
                             else ENTRY_MASKED if (e < 3 or e > 11) else e)
                elif kind == GROUP_FIRST:
                    entry = ENTRY_MASKED if j >= WIN_ROWS else e
                else:
                    entry = ENTRY_MASKED if j < WINDOW_ROWS - WIN_ROWS else e
                assert 0 <= entry < N_BIAS_ENTRIES
                table[kind, j * (GROUP_ROWS // 2) + ip] = entry
    return table


def _lat_attn_kernel(ent_ref, x_ref, yab_ref, qt_ref, g_ref, k_ref, vt_ref, ck_ref, cvt_ref, tbl_ref,
                     mod_ref, w_out_ref, ln_g_ref, ln_b_ref, xo_ref, s_scr, m_scr, ot_scr):
    rb = pl.program_id(1)
    tb = x_ref.shape[0]
    group_tokens = GROUP_ROWS * GRID_W
    window_tokens = WINDOW_ROWS * GRID_W
    window_tiles = window_tokens // LANES
    n_groups = (k_ref.shape[1] // GRID_W) // GROUP_ROWS
    groups_per_block = tb // group_tokens
    n_units = N_HEAD_PAIRS * groups_per_block * 2
    group_bits = groups_per_block.bit_length() - 1
    assert groups_per_block == 1 << group_bits

    def coords(u):
        hp = lax.shift_right_logical(u, 1 + group_bits)
        gi = lax.shift_right_logical(u, 1) & (groups_per_block - 1)
        parity = u & 1
        g = rb * groups_per_block + gi
        first_tile = jnp.clip(2 * g - 2, 0, (GRID_W - WINDOW_ROWS) // 2)
        return hp, gi, parity, g, first_tile

    def score_stage(u, slot):
        hp, gi, parity, g, first_tile = coords(u)
        start = pl.multiple_of(first_tile * LANES, LANES)
        kw = k_ref[hp, pl.ds(start, window_tokens), :]
        qt = qt_ref[hp, gi]
        row_half = lax.shift_right_logical(
            lax.broadcasted_iota(jnp.int32, (LANES, group_tokens), 0), 6)
        qt_h = jnp.where(row_half == parity, qt, jnp.zeros_like(qt))
        key_chunk = 256
        s_loc = jnp.concatenate(
            [jnp.dot(kw[c:c + key_chunk], qt_h, preferred_element_type=F32)
             for c in range(0, window_tokens, key_chunk)], axis=0)
        s_ctx = jnp.dot(ck_ref[hp, 0], qt_h, preferred_element_type=F32)
        head_base = (2 * hp + parity) * N_BIAS_ENTRIES
        kind = jnp.where(g == 0, GROUP_FIRST, jnp.where(g == n_groups - 1, GROUP_LAST, GROUP_INTERIOR))
        bias_rows = []
        for j in range(WINDOW_ROWS):
            tiles = []
            for ip in range(GROUP_ROWS // 2):
                tiles.append(tbl_ref[head_base + ent_ref[kind, j * (GROUP_ROWS // 2) + ip]])
            bias_rows.append(jnp.concatenate(tiles, axis=1))
        s_loc = s_loc + jnp.concatenate(bias_rows, axis=0)
        s_scr[slot, 0:window_tokens, :] = s_loc
        s_scr[slot, window_tokens:, :] = s_ctx
        m_scr[slot] = jnp.maximum(jnp.max(s_loc, axis=0, keepdims=True),
                                  jnp.max(s_ctx, axis=0, keepdims=True))

    def value_stage(u, slot):
        hp, gi, parity, g, first_tile = coords(u)
        p = jnp.exp2(s_scr[slot] - m_scr[slot]).astype(BF16)
        rows = pl.ds(pl.multiple_of(parity * C_HD, C_HD), C_HD)
        vt_tiles = vt_ref[hp, pl.ds(first_tile, window_tiles), rows, :]
        vtw = jnp.concatenate([vt_tiles[n] for n in range(window_tiles)] + [cvt_ref[hp, 0, rows, :]],
                              axis=1)
        lhs = jnp.concatenate([vtw, jnp.ones((16, s_scr.shape[1]), BF16)], axis=0)
        o_t = jnp.dot(lhs, p, preferred_element_type=F32)
        ot_scr[hp, gi, rows, :] = o_t[0:C_HD] / o_t[C_HD:C_HD + 1]

    n_slots = s_scr.shape[0]
    assert n_slots > SCORE_LEAD_LAT and n_units % n_slots == 0

    def pipeline_step(j, carry):
        for i in range(n_slots):
            score_stage(n_slots * j + i + SCORE_LEAD_LAT, (i + SCORE_LEAD_LAT) % n_slots)
            value_stage(n_slots * j + i, i)
        return carry

    gate = mod_ref[0, :, 2 * D_MODEL:3 * D_MODEL]

    def output_stage(gi):
        rows = slice(gi * group_tokens, (gi + 1) * group_tokens)
        attn = jnp.concatenate([ot_scr[t, gi].T for t in range(N_HEAD_PAIRS)], axis=1)
        yc = (attn * g_ref[rows, :].astype(F32)).astype(BF16)
        y = jnp.concatenate([yab_ref[rows, :], yc], axis=1)
        xo_ref[rows, :] = _post_norm(x_ref[rows, :], y, gate, w_out_ref, ln_g_ref, ln_b_ref)

    n_tail = 2 * groups_per_block
    assert (n_units - n_tail) % n_slots == 0
    for u in range(SCORE_LEAD_LAT):
        score_stage(jnp.int32(u), u)
    lax.fori_loop(0, (n_units - n_tail) // n_slots, pipeline_step, 0)
    for u in range(n_units - n_tail, n_units):
        if u + SCORE_LEAD_LAT < n_units:
            score_stage(jnp.int32(u + SCORE_LEAD_LAT), (u + SCORE_LEAD_LAT) % n_slots)
        value_stage(jnp.int32(u), u % n_slots)
        if u % 2 == 1:
            output_stage((u - (n_units - n_tail)) // 2)


def _lat_attn(layer, xs, yab, q, k, vt, gc, ck, cvt, tbl, mod, lw, seq_len):
    n_tok = xs.shape[0]
    n_batch = n_tok // seq_len
    blocks_per_seq = seq_len // ATTN_TOKEN_BLOCK
    past = ck.shape[2]
    group_tokens = GROUP_ROWS * GRID_W
    tok_spec = lambda width: pl.BlockSpec((ATTN_TOKEN_BLOCK, width),
                                          lambda b, r: (b * blocks_per_seq + r, 0))
    return pl.pallas_call(
        _lat_attn_kernel,
        grid=(n_batch, blocks_per_seq),
        in_specs=[
            pl.BlockSpec(memory_space=pltpu.SMEM),
            tok_spec(D_MODEL),
            tok_spec(A_W + B_W),
            pl.BlockSpec((N_HEAD_PAIRS, ATTN_TOKEN_BLOCK // group_tokens, LANES, group_tokens),
                         lambda b, r: (0, b * blocks_per_seq + r, 0, 0)),
            tok_spec(C_W),
            pl.BlockSpec((N_HEAD_PAIRS, seq_len, LANES), lambda b, r: (0, b, 0)),
            pl.BlockSpec((N_HEAD_PAIRS, seq_len // LANES, LANES, LANES), lambda b, r: (0, b, 0, 0)),
            pl.BlockSpec((N_HEAD_PAIRS, 1, past, LANES), lambda b, r: (0, b, 0, 0)),
            pl.BlockSpec((N_HEAD_PAIRS, 1, LANES, past), lambda b, r: (0, b, 0, 0)),
            _layer_spec(tbl.shape[1:], layer),
            pl.BlockSpec((1, 1, 3 * D_MODEL), lambda b, r: (b, 0, 0)),
            _const_spec((D_MODEL, D_MODEL)),
            _const_spec((1, D_MODEL)),
            _const_spec((1, D_MODEL)),
        ],
        out_specs=tok_spec(D_MODEL),
        out_shape=jax.ShapeDtypeStruct((n_tok, D_MODEL), F32),
        scratch_shapes=[
            pltpu.VMEM((4, WINDOW_ROWS * GRID_W + past, group_tokens), F32),
            pltpu.VMEM((4, 1, group_tokens), F32),
            pltpu.VMEM((N_HEAD_PAIRS, ATTN_TOKEN_BLOCK // group_tokens, LANES, group_tokens), F32),
        ],
        compiler_params=pltpu.CompilerParams(
            dimension_semantics=("arbitrary", "arbitrary"),
            vmem_limit_bytes=VMEM_LIMIT_BYTES),
        name=f"lat_attn_{layer}",
    )(jnp.asarray(_bias_entry_table()), xs, yab, q, gc, k, vt, ck, cvt, tbl, mod, lw["w_out"],
      lw["ln_g"], lw["ln_b"])


def _bias_table_kernel(rp_ref, o_ref):
    n_taps = 2 * WIN_COLS - 1
    n_dr = 2 * WIN_ROWS - 1
    cp = lax.broadcasted_iota(jnp.int32, (GRID_W, LANES), 0)
    lane = lax.broadcasted_iota(jnp.int32, (GRID_W, LANES), 1)
    c = lane & (GRID_W - 1)
    cs = jnp.clip(c - WIN_COLS // 2, 0, GRID_W - WIN_COLS)
    valid = (cp >= cs) & (cp < cs + WIN_COLS)
    low = lane < GRID_W
    neg = jnp.full((GRID_W, LANES), NEG_INF, F32)

    def diagonal(head, e):
        row = jnp.broadcast_to(rp_ref[0, head, e:e + 1, :], (GRID_W, LANES))
        return pltpu.roll(row, LANES - n_taps // 2, 1, stride=1, stride_axis=0) * LOG2_E

    for head in range(C_HEADS):
        diag = [diagonal(head, e) for e in range(n_dr)]

        def pair(e_low, e_high):
            lo = neg if e_low is None else diag[e_low]
            hi = neg if e_high is None else diag[e_high]
            return jnp.where(valid, jnp.where(low, lo, hi), NEG_INF)

        for e in range(n_dr + 1):
            o_ref[0, head, e] = pair(e if e < n_dr else None, e - 1 if e >= 1 else None)
        o_ref[0, head, ENTRY_MASKED] = neg
        o_ref[0, head, ENTRY_LOW_ONLY] = pair(3, None)
        o_ref[0, head, ENTRY_HIGH_ONLY] = pair(None, 10)


def _bias_tables(rpb):
    n_dr, n_taps = rpb.shape[2], rpb.shape[3]
    rp = jnp.pad(rpb[..., ::-1], ((0, 0), (0, 0), (0, 16 - n_dr), (0, GRID_W - n_taps)),
                 constant_values=NEG_INF)
    rp = jnp.concatenate([rp, rp], axis=-1)
    tbl = pl.pallas_call(
        _bias_table_kernel,
        grid=(DEPTH,),
        in_specs=[pl.BlockSpec((1, C_HEADS, 16, LANES), lambda l: (l, 0, 0, 0))],
        out_specs=pl.BlockSpec((1, C_HEADS, N_BIAS_ENTRIES, GRID_W, LANES), lambda l: (l, 0, 0, 0, 0)),
        out_shape=jax.ShapeDtypeStruct((DEPTH, C_HEADS, N_BIAS_ENTRIES, GRID_W, LANES), F32),
        compiler_params=pltpu.CompilerParams(dimension_semantics=("arbitrary",)),
        name="bias_tables",
    )(rp)
    return tbl.reshape(DEPTH, C_HEADS * N_BIAS_ENTRIES, GRID_W, LANES)


def _layer_weights(l, w_in_bf, w_out_bf, conv_w, gmlp_ln_g, gmlp_ln_b, ws_pairs, bs_lanes, ln_g, ln_b):
    return {
        "w_in": w_in_bf, "w_out": w_out_bf, "conv_w": conv_w[l],
        "gln_g": gmlp_ln_g[l][None], "gln_b": gmlp_ln_b[l][None],
        "ws": ws_pairs[l], "bs": bs_lanes[l],
        "ln_g": ln_g[l][None], "ln_b": ln_b[l][None],
    }


def kernel(x_prompt, x_sample, cache_k, cache_v, c, c_ctx, w_ada, b_ada, w_in, conv_w, gmlp_ln_g,
           gmlp_ln_b, w_spatial, b_spatial, rpb, w_out, ln_g, ln_b):
    batch, seq, d = x_prompt.shape
    dec_batch, dec_seq, _ = x_sample.shape
    past = cache_k.shape[2]

    n_mod_rows = 8
    cvec = jnp.concatenate(
        [c_ctx[None], c, jnp.zeros((n_mod_rows - 1 - dec_batch, d), F32)], axis=0)
    mod = _modulation(cvec, w_ada, b_ada)

    w_in_bf = w_in[0].astype(BF16)
    w_out_bf = w_out[0].astype(BF16)
    ws_pairs = w_spatial.reshape(DEPTH, 2, 2, CHUNK, CHUNK).transpose(0, 1, 3, 2, 4)
    ws_pairs = ws_pairs.reshape(DEPTH, 2, CHUNK, 2 * CHUNK).astype(BF16)
    bs_lanes = jnp.repeat(b_spatial.reshape(DEPTH, 2, 2, CHUNK).transpose(0, 1, 3, 2), C_HD, axis=-1)

    def pair_major(cache):
        cache = cache.reshape(dec_batch, DEPTH, past, N_HEAD_PAIRS, LANES)
        return cache.transpose(1, 3, 0, 2, 4).astype(BF16)

    ck_all = pair_major(cache_k)
    cvt_all = jnp.swapaxes(pair_major(cache_v), -1, -2)
    tables = _bias_tables(rpb)

    xp = x_prompt.reshape(batch * seq, d)
    xs = x_sample.reshape(dec_batch * dec_seq, d)
    kv = None
    for l in range(DEPTH):
        lw = _layer_weights(l, w_in_bf, w_out_bf, conv_w, gmlp_ln_g, gmlp_ln_b, ws_pairs, bs_lanes,
                            ln_g, ln_b)
        xp, new_k, new_v, w_in_next, w_out_next = _ctx_layer(l, xp, mod[l, 0:1][None], lw, kv, w_in,
                                                             w_out)
        kv = (new_k, new_v)
        mod_lat = mod[l, 1:1 + dec_batch][:, None]
        yab, q, k, vt, gc = _lat_proj(l, xs, mod_lat, lw, dec_seq)
        xs = _lat_attn(l, xs, yab, q, k, vt, gc, ck_all[l], cvt_all[l], tables, mod_lat, lw, dec_seq)
        w_in_bf, w_out_bf = w_in_next, w_out_next

    new_k, new_v = kv
    return (xp.reshape(batch, seq, d), xs.reshape(dec_batch, dec_seq, d),
            new_k.reshape(batch, DEPTH, seq, C_HEADS, C_HD),
            new_v.reshape(batch, DEPTH, seq, C_HEADS, C_HD))
```
